```python
import math
import jax, jax.numpy as jnp
from jax import lax
import numpy as np

D_MODEL = 1024
BATCH = 16
SEQ = 4096
DEPTH = 1
DEC_BATCH = 128
DEC_SEQ = 4
PAST_LEN = 8192
PAGE_SIZE = 128

MIX_WIDTH = D_MODEL
M_WIDTH = MIX_WIDTH // 2
M_HEADS = 4
M_HD = M_WIDTH // M_HEADS
CONV_W = 4
M_CHUNK = 128
A_WIDTH = MIX_WIDTH - M_WIDTH
A_HEADS = 4
A_VD = A_WIDTH // A_HEADS
A_QD = A_VD // 2
Q_BLOCK = 128
D_FF = -(-8 * D_MODEL // (3 * 256)) * 256
ALPHA = (2.0 * DEPTH) ** 0.25
BETA = (8.0 * DEPTH) ** -0.25
LN_EPS = 1e-5
F32 = jnp.float32

kernel_name = "hymba_mlstm_diffattn_alibi_deepnorm_step"


def _layer_norm(x, g, b):
    x32 = x.astype(F32)
    mu = jnp.mean(x32, -1, keepdims=True)
    var = jnp.mean(jnp.square(x32 - mu), -1, keepdims=True)
    return (x32 - mu) * lax.rsqrt(var + LN_EPS) * g.astype(F32) + b.astype(F32)


def _rms_heads(h, g):
    h32 = h.astype(F32)
    return h32 * lax.rsqrt(jnp.mean(jnp.square(h32), -1, keepdims=True) + LN_EPS) * g.astype(F32)


def _alibi_slopes():
    return jnp.exp2(-8.0 * jnp.arange(1, A_HEADS + 1, dtype=F32) / A_HEADS)


def _lambda_init(layer):
    return 0.8 - 0.6 * math.exp(-0.3 * layer)


def _causal_conv(u, buf, w, b):
    t = u.shape[1]
    up = jnp.concatenate([buf.astype(u.dtype), u], axis=1)
    y = b + w[0] * up[:, 0:t]
    for j in range(1, CONV_W):
        y = y + w[j] * up[:, j:j + t]
    return jax.nn.silu(y), up[:, t:]


def _mlstm_chunk(carry, inp):
    C, n, m = carry
    q, k, v, ig, lf = inp
    L = q.shape[2]
    b = jnp.cumsum(lf, axis=-1)
    causal = jnp.tril(jnp.ones((L, L), dtype=bool))
    log_d = jnp.where(causal, b[..., :, None] - b[..., None, :] + ig[..., None, :], -jnp.inf)
    m_inter = b + m[..., None]
    m_t = jnp.maximum(m_inter, jnp.max(log_d, -1))
    d_w = jnp.exp(log_d - m_t[..., None])
    inter = jnp.exp(m_inter - m_t)
    s = jnp.einsum('bhtd,bhsd->bhts', q, k) * d_w
    num = jnp.einsum('bhts,bhsd->bhtd', s, v) + inter[..., None] * jnp.einsum('bhvk,bhtk->bhtv', C, q)
    den = jnp.sum(s, -1) + inter * jnp.einsum('bhk,bhtk->bht', n, q)
    h = num / jnp.maximum(jnp.abs(den), jnp.exp(-m_t))[..., None]
    m_new = m_t[..., -1]
    w_end = jnp.exp(b[..., -1:] - b + ig - m_new[..., None])
    decay = jnp.exp(b[..., -1] + m - m_new)
    C_new = decay[..., None, None] * C + jnp.einsum('bhs,bhsv,bhsk->bhvk', w_end, v, k)
    n_new = decay[..., None] * n + jnp.einsum('bhs,bhsk->bhk', w_end, k)
    return (C_new, n_new, m_new), h


def _mlstm_run(q, k, v, ig, lf, C, n, m):
    B, H, T, d = q.shape
    L = M_CHUNK if T % M_CHUNK == 0 else T
    nc = T // L

    def chunks(a):
        return jnp.moveaxis(a.reshape((B, H, nc, L) + a.shape[3:]), 2, 0)

    (C, n, m), h = lax.scan(_mlstm_chunk, (C, n, m), (chunks(q), chunks(k), chunks(v), chunks(ig), chunks(lf)))
    return jnp.moveaxis(h, 0, 2).reshape(B, H, T, d), C, n, m


def _diff_attend(q, k, v, q_pos, k_pos, lam, slopes):
    s = jnp.einsum('bqhcd,bkhcd->bhcqk', q.astype(F32), k.astype(F32)) * (A_QD ** -0.5)
    dist = (q_pos[:, None] - k_pos[None, :]).astype(F32)
    s = s - slopes[None, :, None, None, None] * dist
    s = jnp.where(dist >= 0, s, -jnp.inf)
    p = jax.nn.softmax(s, axis=-1)
    a = p[:, :, 0] - lam * p[:, :, 1]
    return jnp.einsum('bhqk,bkhv->bqhv', a, v.astype(F32))


def _token_mixer(x, conv_buf, C, n, m, k_past, v_past, lam, lam_init,
                 w_in, b_ig, b_fg, conv_w, conv_b, m_norm_g, a_norm_g, w_out):
    B, T, _ = x.shape
    sizes = [2 * M_WIDTH, M_WIDTH, M_WIDTH, M_HEADS, M_HEADS, A_WIDTH, A_WIDTH, A_WIDTH]
    idx = [int(i) for i in np.cumsum(sizes)[:-1]]
    proj = x @ w_in
    qk_pre, v_m, o_m, i_pre, f_pre, q_a, k_a, v_a = jnp.split(proj, idx, axis=-1)

    qk, conv_new = _causal_conv(qk_pre, conv_buf, conv_w, conv_b)
    q_m, k_m = jnp.split(qk, 2, axis=-1)

    def heads(a):
        return a.reshape(B, T, M_HEADS, M_HD).transpose(0, 2, 1, 3).astype(F32)

    ig = (i_pre + b_ig).astype(F32).transpose(0, 2, 1)
    lf = jax.nn.log_sigmoid((f_pre + b_fg).astype(F32)).transpose(0, 2, 1)
    h, C, n, m = _mlstm_run(heads(q_m), heads(k_m) * (M_HD ** -0.5), heads(v_m), ig, lf, C, n, m)
    h = _rms_heads(h.transpose(0, 2, 1, 3), m_norm_g.reshape(M_HEADS, M_HD))
    h_m = (jax.nn.sigmoid(o_m.astype(F32)).reshape(B, T, M_HEADS, M_HD) * h).reshape(B, T, M_WIDTH)

    qa = q_a.reshape(B, T, A_HEADS, 2, A_QD)
    ka = k_a.reshape(B, T, A_HEADS, 2, A_QD)
    va = v_a.reshape(B, T, A_HEADS, A_VD)
    slopes = _alibi_slopes()
    if k_past is None:
        nb = T // Q_BLOCK
        qb = jnp.moveaxis(qa.reshape(B, nb, Q_BLOCK, A_HEADS, 2, A_QD), 1, 0)
        starts = jnp.arange(nb, dtype=jnp.int32) * Q_BLOCK
        k_pos = jnp.arange(T, dtype=jnp.int32)
        o = lax.map(lambda a: _diff_attend(a[0], ka, va, a[1] + jnp.arange(Q_BLOCK, dtype=jnp.int32),
                                           k_pos, lam, slopes), (qb, starts))
        o = jnp.moveaxis(o, 0, 1).reshape(B, T, A_HEADS, A_VD)
    else:
        past = k_past.shape[1]
        k_all = jnp.concatenate([k_past.astype(ka.dtype), ka], axis=1)
        v_all = jnp.concatenate([v_past.astype(va.dtype), va], axis=1)
        q_pos = past + jnp.arange(T, dtype=jnp.int32)
        k_pos = jnp.arange(past + T, dtype=jnp.int32)
        o = _diff_attend(qa, k_all, v_all, q_pos, k_pos, lam, slopes)
    h_a = (_rms_heads(o, a_norm_g) * (1.0 - lam_init)).reshape(B, T, A_WIDTH)

    out = jnp.concatenate([h_m, h_a], axis=-1).astype(x.dtype) @ w_out
    k_rows = k_a.reshape(B, T, A_HEADS, 2 * A_QD)
    return out, k_rows, va, C, n, m, conv_new


def _layer(x, conv_buf, C, n, m, k_past, v_past, lam, lam_init,
           w_in, b_ig, b_fg, conv_w, conv_b, m_norm_g, a_norm_g, w_out,
           ln1_g, ln1_b, w_gate, w_up, w_down, ln2_g, ln2_b):
    mix, k_rows, v_rows, C, n, m, conv_new = _token_mixer(
        x, conv_buf, C, n, m, k_past, v_past, lam, lam_init,
        w_in, b_ig, b_fg, conv_w, conv_b, m_norm_g, a_norm_g, w_out)
    x1 = _layer_norm(ALPHA * x.astype(F32) + mix.astype(F32), ln1_g, ln1_b).astype(x.dtype)
    ffn = (jax.nn.silu(x1 @ w_gate) * (x1 @ w_up)) @ w_down
    x2 = _layer_norm(ALPHA * x1.astype(F32) + ffn.astype(F32), ln2_g, ln2_b).astype(x.dtype)
    return x2, k_rows, v_rows, C, n, m, conv_new


def setup_inputs(seed: int = 0) -> dict:
    key = jax.random.key(seed)
    ks = jax.random.split(key, 32)
    n_pages = PAST_LEN // PAGE_SIZE
    n_used = DEC_BATCH * n_pages
    n_phys = n_used + n_used // 4
    in_cols = 4 * M_WIDTH + 2 * M_HEADS + 3 * A_WIDTH
    nrm = jax.random.normal
    page_table = jax.random.permutation(ks[0], n_phys)[:n_used].reshape(DEC_BATCH, n_pages).astype(jnp.int32)
    return {
        'x_prompt': nrm(ks[1], (BATCH, SEQ, D_MODEL), F32),
        'x_sample': nrm(ks[2], (DEC_BATCH, DEC_SEQ, D_MODEL), F32),
        'cache_k': nrm(ks[3], (DEPTH, n_phys, PAGE_SIZE, A_HEADS, 2 * A_QD), F32),
        'cache_v': nrm(ks[4], (DEPTH, n_phys, PAGE_SIZE, A_HEADS, A_VD), F32),
        'page_table': page_table,
        'state_C': 0.1 * nrm(ks[5], (DEPTH, DEC_BATCH, M_HEADS, M_HD, M_HD), F32),
        'state_n': nrm(ks[6], (DEPTH, DEC_BATCH, M_HEADS, M_HD), F32),
        'state_m': nrm(ks[7], (DEPTH, DEC_BATCH, M_HEADS), F32),
        'state_conv': nrm(ks[8], (DEPTH, DEC_BATCH, CONV_W - 1, 2 * M_WIDTH), F32),
        'w_in': nrm(ks[9], (DEPTH, D_MODEL, in_cols), F32) * D_MODEL ** -0.5,
        'b_ig': 0.1 * nrm(ks[10], (DEPTH, M_HEADS), F32),
        'b_fg': jnp.linspace(3.0, 6.0, M_HEADS, dtype=F32)[None, :] + 0.01 * nrm(ks[11], (DEPTH, M_HEADS), F32),
        'conv_w': nrm(ks[12], (DEPTH, CONV_W, 2 * M_WIDTH), F32) * CONV_W ** -0.5,
        'conv_b': 0.01 * nrm(ks[13], (DEPTH, 2 * M_WIDTH), F32),
        'm_norm_g': 1.0 + 0.01 * nrm(ks[14], (DEPTH, M_WIDTH), F32),
        'lam_q1': 0.1 * nrm(ks[15], (DEPTH, A_QD), F32),
        'lam_k1': 0.1 * nrm(ks[16], (DEPTH, A_QD), F32),
        'lam_q2': 0.1 * nrm(ks[17], (DEPTH, A_QD), F32),
        'lam_k2': 0.1 * nrm(ks[18], (DEPTH, A_QD), F32),
        'a_norm_g': 1.0 + 0.01 * nrm(ks[19], (DEPTH, A_VD), F32),
        'w_out': nrm(ks[20], (DEPTH, MIX_WIDTH, D_MODEL), F32) * (MIX_WIDTH ** -0.5) * BETA,
        'ln1_g': 1.0 + 0.01 * nrm(ks[21], (DEPTH, D_MODEL), F32),
        'ln1_b': 0.01 * nrm(ks[22], (DEPTH, D_MODEL), F32),
        'w_gate': nrm(ks[23], (DEPTH, D_MODEL, D_FF), F32) * D_MODEL ** -0.5,
        'w_up': nrm(ks[24], (DEPTH, D_MODEL, D_FF), F32) * D_MODEL ** -0.5,
        'w_down': nrm(ks[25], (DEPTH, D_FF, D_MODEL), F32) * (D_FF ** -0.5) * BETA,
        'ln2_g': 1.0 + 0.01 * nrm(ks[26], (DEPTH, D_MODEL), F32),
        'ln2_b': 0.01 * nrm(ks[27], (DEPTH, D_MODEL), F32),
    }


def reference(x_prompt, x_sample, cache_k, cache_v, page_table, state_C, state_n, state_m, state_conv,
              w_in, b_ig, b_fg, conv_w, conv_b, m_norm_g, lam_q1, lam_k1, lam_q2, lam_k2, a_norm_g,
              w_out, ln1_g, ln1_b, w_gate, w_up, w_down, ln2_g, ln2_b):
    B = x_prompt.shape[0]
    DB = x_sample.shape[0]
    past = page_table.shape[1] * cache_k.shape[2]
    hp, hs = x_prompt, x_sample
    kp_l, vp_l, ks_l, vs_l = [], [], [], []
    cp_l, np_l, mp_l, cvp_l, cs_l, ns_l, ms_l, cvs_l = [], [], [], [], [], [], [], []
    for l in range(DEPTH):
        lam_init = _lambda_init(l)
        lam = (jnp.exp(jnp.sum(lam_q1[l].astype(F32) * lam_k1[l].astype(F32)))
               - jnp.exp(jnp.sum(lam_q2[l].astype(F32) * lam_k2[l].astype(F32))) + lam_init)
        lw = (w_in[l], b_ig[l], b_fg[l], conv_w[l], conv_b[l], m_norm_g[l], a_norm_g[l], w_out[l],
              ln1_g[l], ln1_b[l], w_gate[l], w_up[l], w_down[l], ln2_g[l], ln2_b[l])
        hp, kp, vp, cp, n_p, mp, cvp = _layer(
            hp, jnp.zeros((B, CONV_W - 1, 2 * M_WIDTH), hp.dtype),
            jnp.zeros((B, M_HEADS, M_HD, M_HD), F32), jnp.zeros((B, M_HEADS, M_HD), F32),
            jnp.zeros((B, M_HEADS), F32), None, None, lam, lam_init, *lw)
        k_past = cache_k[l][page_table].reshape(DB, past, A_HEADS, 2, A_QD)
        v_past = cache_v[l][page_table].reshape(DB, past, A_HEADS, A_VD)
        hs, ksm, vsm, cs, n_s, ms, cvs = _layer(
            hs, state_conv[l], state_C[l].astype(F32), state_n[l].astype(F32), state_m[l].astype(F32),
            k_past, v_past, lam, lam_init, *lw)
        kp_l.append(kp); vp_l.append(vp); ks_l.append(ksm); vs_l.append(vsm)
        cp_l.append(cp); np_l.append(n_p); mp_l.append(mp); cvp_l.append(cvp)
        cs_l.append(cs); ns_l.append(n_s); ms_l.append(ms); cvs_l.append(cvs)
    return (hp, hs,
            jnp.stack(kp_l), jnp.stack(vp_l), jnp.stack(ks_l), jnp.stack(vs_l),
            jnp.stack(cp_l), jnp.stack(np_l), jnp.stack(mp_l), jnp.stack(cvp_l),
            jnp.stack(cs_l), jnp.stack(ns_l), jnp.stack(ms_l), jnp.stack(cvs_l))
```

```python
import functools
import math

import jax
import jax.numpy as jnp
from jax import lax
from jax.experimental import pallas as pl
from jax.experimental.pallas import tpu as pltpu

F32 = jnp.float32
BF16 = jnp.bfloat16

LN_EPS = 1e-5
M_CHUNK = 128
LANES = 128
SUBLANES = 8
ROW_TILE = 512
ATTN_TILE = 256
PAGES_PER_STEP = 16
VMEM_LIMIT = 56 * 1024 * 1024

_NT = (((1,), (1,)), ((), ()))
_TN = (((0,), (0,)), ((), ()))


def _sigmoid(x):
    return 1.0 / (1.0 + jnp.exp(-x))


def _log_sigmoid(x):
    return jnp.minimum(x, 0.0) - jnp.log(1.0 + jnp.exp(-jnp.abs(x)))


def _params(*sem):
    return pltpu.CompilerParams(dimension_semantics=sem, vmem_limit_bytes=VMEM_LIMIT)


def _proj_kernel(x_ref, wm_ref, wg_ref, qk_ref, vm_ref, om_ref, g_ref, gt_ref,
                 q_ref, k_ref, v_ref, kb_ref, vb_ref, *, mw, aw, q_scale):
    xb = x_ref[...].astype(BF16)

    def seg(lo, n):
        return jnp.dot(xb, wm_ref[:, lo:lo + n], preferred_element_type=F32)

    qk_ref[...] = seg(0, 2 * mw)
    vm_ref[...] = seg(2 * mw, mw).astype(BF16)
    om_ref[...] = seg(3 * mw, mw).astype(BF16)
    q_ref[...] = (seg(4 * mw, aw) * q_scale).astype(BF16)
    k = seg(4 * mw + aw, aw)
    k_ref[...] = k
    kb_ref[...] = k.astype(BF16)
    v = seg(4 * mw + 2 * aw, aw)
    v_ref[...] = v
    vb_ref[...] = v.astype(BF16)
    g = jnp.dot(xb, wg_ref[...], preferred_element_type=F32)
    g_ref[...] = g
    gt_ref[...] = g.T[:SUBLANES, :]


def _project(x2d, w_main, w_gate, mw, aw, q_scale):
    rows, d = x2d.shape
    tm = min(ROW_TILE, rows)
    assert rows % tm == 0
    row = lambda n: pl.BlockSpec((tm, n), lambda i: (i, 0))
    const = lambda a: pl.BlockSpec(a.shape, lambda i: (0,) * a.ndim, pipeline_mode=pl.Buffered(1))
    out_shape = (
        jax.ShapeDtypeStruct((rows, 2 * mw), F32),
        jax.ShapeDtypeStruct((rows, mw), BF16),
        jax.ShapeDtypeStruct((rows, mw), BF16),
        jax.ShapeDtypeStruct((rows, LANES), F32),
        jax.ShapeDtypeStruct((SUBLANES, rows), F32),
        jax.ShapeDtypeStruct((rows, aw), BF16),
        jax.ShapeDtypeStruct((rows, aw), F32),
        jax.ShapeDtypeStruct((rows, aw), F32),
        jax.ShapeDtypeStruct((rows, aw), BF16),
        jax.ShapeDtypeStruct((rows, aw), BF16),
    )
    out_specs = (row(2 * mw), row(mw), row(mw), row(LANES),
                 pl.BlockSpec((SUBLANES, tm), lambda i: (0, i)),
                 row(aw), row(aw), row(aw), row(aw), row(aw))
    return pl.pallas_call(
        functools.partial(_proj_kernel, mw=mw, aw=aw, q_scale=q_scale),
        grid=(rows // tm,),
        in_specs=[row(d), const(w_main), const(w_gate)],
        out_specs=out_specs,
        out_shape=out_shape,
        compiler_params=_params("parallel"),
        name="in_proj",
    )(x2d, w_main, w_gate)


def _conv_silu(ext_ref, length, cw_ref, cb_ref):
    width = cw_ref.shape[0]
    y = cb_ref[...] + cw_ref[width - 1:width, :] * ext_ref[pl.ds(SUBLANES, length), :]
    for j in range(width - 1):
        y = y + cw_ref[j:j + 1, :] * ext_ref[pl.ds(SUBLANES - (width - 1) + j, length), :]
    return y * _sigmoid(y)


def _mlstm_chunk(q, k, v, ig_col, ig_row, lf_col, lf_row, C, n, m):
    L = q.shape[0]
    row = lax.broadcasted_iota(jnp.int32, (L, L), 0)
    col = lax.broadcasted_iota(jnp.int32, (L, L), 1)
    tril = col <= row
    b_col = jnp.sum(jnp.where(tril, lf_row, 0.0), axis=1, keepdims=True)
    b_row = jnp.sum(jnp.where(row <= col, lf_col, 0.0), axis=0, keepdims=True)
    log_d = jnp.where(tril, b_col - b_row + ig_row, -jnp.inf)
    m_inter = b_col + m
    m_t = jnp.maximum(m_inter, jnp.max(log_d, axis=1, keepdims=True))
    d_w = jnp.exp(log_d - m_t)
    inter = jnp.exp(m_inter - m_t)
    qb, kb, vb = q.astype(BF16), k.astype(BF16), v.astype(BF16)
    s = lax.dot_general(qb, kb, _NT, preferred_element_type=F32) * d_w
    num = (jnp.dot(s.astype(BF16), vb, preferred_element_type=F32)
           + inter * lax.dot_general(qb, C.astype(BF16), _NT, preferred_element_type=F32))
    den = jnp.sum(s, axis=1, keepdims=True) + inter * jnp.sum(q * n, axis=1, keepdims=True)
    h = num / jnp.maximum(jnp.abs(den), jnp.exp(-m_t))
    m_new = m_t[L - 1:L, :]
    b_last = b_col[L - 1:L, :]
    w_end = jnp.exp(b_last - b_col + ig_col - m_new)
    decay = jnp.exp(b_last + m - m_new)
    vw = (w_end * v.astype(F32)).astype(BF16)
    C_new = decay * C + lax.dot_general(vw, kb, _TN, preferred_element_type=F32)
    n_new = decay * n + jnp.sum(w_end * k, axis=0, keepdims=True)
    return h, C_new, n_new, m_new


def _mlstm_heads(y, vm, om, gc, gr, ng_ref, state, hm_ref, C_ref, n_ref, m_ref, *, heads, hd):
    mw = heads * hd
    lf_c = _log_sigmoid(gc)
    lf_r = _log_sigmoid(gr)
    k_scale = hd ** -0.5
    for h in range(heads):
        sl = slice(h * hd, (h + 1) * hd)
        C, n, m = state(h)
        hh, C_new, n_new, m_new = _mlstm_chunk(
            y[:, sl], y[:, mw + h * hd:mw + (h + 1) * hd] * k_scale, vm[:, sl],
            gc[:, h:h + 1], gr[h:h + 1, :], lf_c[:, heads + h:heads + h + 1], lf_r[heads + h:heads + h + 1, :],
            C, n, m)
        C_ref[h] = C_new
        n_ref[h:h + 1, :] = n_new
        m_ref[h:h + 1, :] = jnp.broadcast_to(m_new, (1, m_ref.shape[1]))
        hn = hh * lax.rsqrt(jnp.mean(hh * hh, axis=1, keepdims=True) + LN_EPS) * ng_ref[:, sl]
        hm_ref[:, sl] = (_sigmoid(om[:, sl].astype(F32)) * hn).astype(hm_ref.dtype)


def _mlstm_prompt_kernel(qk_ref, vm_ref, om_ref, g_ref, gt_ref, cw_ref, cb_ref, gbr_ref, gbc_ref, ng_ref,
                         hm_ref, C_ref, n_ref, m_ref, tail_ref, ext_ref, *, heads, hd):
    c = pl.program_id(1)
    L = qk_ref.shape[0]

    @pl.when(c == 0)
    def _():
        C_ref[...] = jnp.zeros_like(C_ref)
        n_ref[...] = jnp.zeros_like(n_ref)
        m_ref[...] = jnp.zeros_like(m_ref)
        ext_ref[0:SUBLANES, :] = jnp.zeros((SUBLANES, ext_ref.shape[1]), F32)

    u = qk_ref[...]
    ext_ref[pl.ds(SUBLANES, L), :] = u
    y = _conv_silu(ext_ref, L, cw_ref, cb_ref)
    ext_ref[0:SUBLANES, :] = u[L - SUBLANES:L, :]
    tail_ref[...] = u[L - SUBLANES:L, :]
    gc = g_ref[...] + gbr_ref[...]
    gr = gt_ref[...] + gbc_ref[...]
    state = lambda h: (C_ref[h], n_ref[h:h + 1, :], m_ref[h:h + 1, 0:1])
    _mlstm_heads(y, vm_ref[...], om_ref[...], gc, gr, ng_ref, state, hm_ref, C_ref, n_ref, m_ref,
                 heads=heads, hd=hd)


def _mlstm_prompt(qk, vm, om, g, gt, conv_w, conv_b, gb_row, gb_col, norm_g, batch, seq, heads, hd):
    mw = heads * hd
    L = M_CHUNK if seq % M_CHUNK == 0 else seq
    nc = seq // L
    assert L % SUBLANES == 0 and 2 * heads <= SUBLANES and conv_w.shape[0] - 1 <= SUBLANES
    const = lambda a: pl.BlockSpec(a.shape, lambda b, c: (0,) * a.ndim)
    seq_spec = lambda n: pl.BlockSpec((None, L, n), lambda b, c: (b, c, 0))
    out_shape = (
        jax.ShapeDtypeStruct((batch, seq, mw), BF16),
        jax.ShapeDtypeStruct((batch, heads, hd, hd), F32),
        jax.ShapeDtypeStruct((batch, heads, hd), F32),
        jax.ShapeDtypeStruct((batch, heads, LANES), F32),
        jax.ShapeDtypeStruct((batch, SUBLANES, 2 * mw), F32),
    )
    out_specs = (
        seq_spec(mw),
        pl.BlockSpec((None, heads, hd, hd), lambda b, c: (b, 0, 0, 0)),
        pl.BlockSpec((None, heads, hd), lambda b, c: (b, 0, 0)),
        pl.BlockSpec((None, heads, LANES), lambda b, c: (b, 0, 0)),
        pl.BlockSpec((None, SUBLANES, 2 * mw), lambda b, c: (b, 0, 0)),
    )
    return pl.pallas_call(
        functools.partial(_mlstm_prompt_kernel, heads=heads, hd=hd),
        grid=(batch, nc),
        in_specs=[seq_spec(2 * mw), seq_spec(mw), seq_spec(mw), seq_spec(LANES),
                  pl.BlockSpec((SUBLANES, L), lambda b, c: (0, b * nc + c)),
                  const(conv_w), const(conv_b), const(gb_row), const(gb_col), const(norm_g)],
        out_specs=out_specs,
        out_shape=out_shape,
        scratch_shapes=[pltpu.VMEM((L + SUBLANES, 2 * mw), F32)],
        compiler_params=_params("parallel", "arbitrary"),
        name="mlstm_prompt",
    )(qk.reshape(batch, seq, 2 * mw), vm.reshape(batch, seq, mw), om.reshape(batch, seq, mw),
      g.reshape(batch, seq, LANES), gt, conv_w, conv_b, gb_row, gb_col, norm_g)


def _mlstm_sample_kernel(qk_ref, vm_ref, om_ref, g_ref, gt_ref, conv_ref, C0_ref, n0_ref, m0_ref,
                         cw_ref, cb_ref, gbr_ref, gbc_ref, ng_ref,
                         hm_ref, C_ref, n_ref, m_ref, ext_ref, *, heads, hd):
    T = qk_ref.shape[0]
    hist = conv_ref.shape[0]
    ext_ref[pl.ds(SUBLANES - hist, hist), :] = conv_ref[...]
    ext_ref[pl.ds(SUBLANES, T), :] = qk_ref[...]
    y = _conv_silu(ext_ref, T, cw_ref, cb_ref)
    gc = g_ref[...] + gbr_ref[...]
    gr = gt_ref[...] + gbc_ref[...]
    state = lambda h: (C0_ref[h], n0_ref[h:h + 1, :], m0_ref[:, h:h + 1])
    _mlstm_heads(y, vm_ref[...], om_ref[...], gc, gr, ng_ref, state, hm_ref, C_ref, n_ref, m_ref,
                 heads=heads, hd=hd)


def _mlstm_sample(qk, vm, om, g, gt, state_conv, state_C, state_n, state_m, layer,
                  conv_w, conv_b, gb_row, gb_col, norm_g, batch, T, heads, hd):
    mw = heads * hd
    hist = state_conv.shape[-2]
    assert T <= M_CHUNK and hist == conv_w.shape[0] - 1 and hist <= SUBLANES and T <= SUBLANES
    gt3 = gt.reshape(SUBLANES, batch, T).transpose(1, 0, 2)
    const = lambda a: pl.BlockSpec(a.shape, lambda b: (0,) * a.ndim)
    tok = lambda n: pl.BlockSpec((None, T, n), lambda b: (b, 0, 0))
    lb = lambda b: layer * batch + b
    out_shape = (
        jax.ShapeDtypeStruct((batch, T, mw), F32),
        jax.ShapeDtypeStruct((batch, heads, hd, hd), F32),
        jax.ShapeDtypeStruct((batch, heads, hd), F32),
        jax.ShapeDtypeStruct((batch, heads, LANES), F32),
    )
    out_specs = (
        tok(mw),
        pl.BlockSpec((None, heads, hd, hd), lambda b: (b, 0, 0, 0)),
        pl.BlockSpec((None, heads, hd), lambda b: (b, 0, 0)),
        pl.BlockSpec((None, heads, LANES), lambda b: (b, 0, 0)),
    )
    depth = state_C.shape[0]
    return pl.pallas_call(
        functools.partial(_mlstm_sample_kernel, heads=heads, hd=hd),
        grid=(batch,),
        in_specs=[tok(2 * mw), tok(mw), tok(mw), tok(LANES),
                  pl.BlockSpec((None, SUBLANES, T), lambda b: (b, 0, 0)),
                  pl.BlockSpec((None, hist, 2 * mw), lambda b: (lb(b), 0, 0)),
                  pl.BlockSpec((None, heads, hd, hd), lambda b: (lb(b), 0, 0, 0)),
                  pl.BlockSpec((None, heads, hd), lambda b: (lb(b), 0, 0)),
                  pl.BlockSpec((None, 1, heads), lambda b: (lb(b), 0, 0)),
                  const(conv_w), const(conv_b), const(gb_row), const(gb_col), const(norm_g)],
        out_specs=out_specs,
        out_shape=out_shape,
        scratch_shapes=[pltpu.VMEM((2 * SUBLANES, 2 * mw), F32)],
        compiler_params=_params("parallel"),
        name="mlstm_sample",
    )(qk.reshape(batch, T, 2 * mw), vm.reshape(batch, T, mw), om.reshape(batch, T, mw),
      g.reshape(batch, T, LANES), gt3,
      state_conv.reshape(depth * batch, hist, 2 * mw), state_C.reshape(depth * batch, heads, hd, hd),
      state_n.reshape(depth * batch, heads, hd), state_m.reshape(depth * batch, 1, heads),
      conv_w, conv_b, gb_row, gb_col, norm_g)


def _lambda(lamv_ref, lam_init):
    a = jnp.sum(lamv_ref[0:1, :] * lamv_ref[1:2, :], axis=1, keepdims=True)
    b = jnp.sum(lamv_ref[2:3, :] * lamv_ref[3:4, :], axis=1, keepdims=True)
    return jnp.exp(a) - jnp.exp(b) + lam_init


def _flash_update(s, v, m_sc, l_sc, acc_sc):
    m_prev = m_sc[...]
    m_new = jnp.maximum(m_prev, jnp.max(s, axis=1, keepdims=True))
    alpha = jnp.exp(m_prev - m_new)
    p = jnp.exp(s - m_new)
    l_sc[...] = alpha * l_sc[...] + jnp.sum(p, axis=1, keepdims=True)
    acc_sc[...] = alpha * acc_sc[...] + jnp.dot(p.astype(BF16), v, preferred_element_type=F32)
    m_sc[...] = m_new


def _attn_prompt_kernel(slopes_ref, q_ref, k_ref, v_ref, lamv_ref, g_ref, o_ref, m_sc, l_sc, acc_sc,
                        *, tile, qd, lam_init):
    h = pl.program_id(1)
    i = pl.program_id(2)
    slope = slopes_ref[h]
    q = q_ref[...]
    lane = lax.broadcasted_iota(jnp.int32, q.shape, 1)
    zero = jnp.zeros_like(q)
    qs = jnp.concatenate([jnp.where(lane < qd, q, zero), jnp.where(lane >= qd, q, zero)], axis=0)
    m_sc[...] = jnp.full_like(m_sc, -jnp.inf)
    l_sc[...] = jnp.zeros_like(l_sc)
    acc_sc[...] = jnp.zeros_like(acc_sc)
    colf = lax.broadcasted_iota(jnp.int32, (1, tile), 1).astype(F32)

    def step(j, masked):
        start = pl.multiple_of(j * tile, tile)
        k = k_ref[pl.ds(start, tile), :]
        v = v_ref[pl.ds(start, tile), :]
        s = lax.dot_general(qs, k, _NT, preferred_element_type=F32)
        s = s + slope * (colf + ((j - i) * tile).astype(F32))
        if masked:
            r = lax.broadcasted_iota(jnp.int32, s.shape, 0)
            r = jnp.where(r >= tile, r - tile, r)
            cc = lax.broadcasted_iota(jnp.int32, s.shape, 1)
            s = jnp.where(cc <= r, s, -jnp.inf)
        _flash_update(s, v, m_sc, l_sc, acc_sc)

    def body(j, carry):
        step(j, False)
        return carry

    lax.fori_loop(0, i, body, 0)
    step(i, True)

    lam = _lambda(lamv_ref, lam_init)
    o = acc_sc[...] / l_sc[...]
    o = o[:tile, :] - lam * o[tile:, :]
    o = o * lax.rsqrt(jnp.mean(o * o, axis=1, keepdims=True) + LN_EPS) * g_ref[...] * (1.0 - lam_init)
    o_ref[...] = o.astype(o_ref.dtype)


def _attn_prompt(q, kb, vb, slopes, lamv, norm_g, batch, seq, heads, vd, lam_init):
    tile = min(ATTN_TILE, seq)
    assert seq % tile == 0 and vd == LANES
    aw = heads * vd
    const = lambda a: pl.BlockSpec(a.shape, lambda b, h, i: (0,) * a.ndim)
    return pl.pallas_call(
        functools.partial(_attn_prompt_kernel, tile=tile, qd=vd // 2, lam_init=lam_init),
        grid=(batch, heads, seq // tile),
        in_specs=[pl.BlockSpec(memory_space=pltpu.SMEM),
                  pl.BlockSpec((None, tile, vd), lambda b, h, i: (b, i, h)),
                  pl.BlockSpec((None, seq, vd), lambda b, h, i: (b, 0, h)),
                  pl.BlockSpec((None, seq, vd), lambda b, h, i: (b, 0, h)),
                  const(lamv), const(norm_g)],
        out_specs=pl.BlockSpec((None, tile, vd), lambda b, h, i: (b, i, h)),
        out_shape=jax.ShapeDtypeStruct((batch, seq, aw), BF16),
        scratch_shapes=[pltpu.VMEM((2 * tile, 1), F32), pltpu.VMEM((2 * tile, 1), F32),
                        pltpu.VMEM((2 * tile, vd), F32)],
        compiler_params=_params("parallel", "parallel", "arbitrary"),
        name="attn_prompt",
    )(slopes, q.reshape(batch, seq, aw), kb.reshape(batch, seq, aw), vb.reshape(batch, seq, aw), lamv, norm_g)


def _attn_decode_kernel(pt_ref, q2_ref, kn_ref, vn_ref, slope_ref, lamv_ref, g_ref, *rest,
                        pages, page, past, heads, vd, T, lam_init):
    k_refs = rest[:pages]
    v_refs = rest[pages:2 * pages]
    o_ref, m_sc, l_sc, acc_sc = rest[2 * pages:]
    c = pl.program_id(1)
    nk = pages * page

    @pl.when(c == 0)
    def _():
        m_sc[...] = jnp.full_like(m_sc, -jnp.inf)
        l_sc[...] = jnp.zeros_like(l_sc)
        acc_sc[...] = jnp.zeros_like(acc_sc)

    q2 = q2_ref[...]
    slope = slope_ref[...]
    kc = jnp.concatenate([r[...].astype(BF16) for r in k_refs], axis=0)
    vc = jnp.concatenate([r[...].astype(BF16) for r in v_refs], axis=0)
    s = lax.dot_general(q2, kc, _NT, preferred_element_type=F32)
    pos = (lax.broadcasted_iota(jnp.int32, (1, nk), 1) + (c * nk - past)).astype(F32)
    _flash_update(s + slope * pos, vc, m_sc, l_sc, acc_sc)

    @pl.when(c == pl.num_programs(1) - 1)
    def _():
        rows = q2.shape[0]
        kn = kn_ref[...]
        sn = lax.dot_general(q2, kn, _NT, preferred_element_type=F32)
        tk = lax.broadcasted_iota(jnp.int32, sn.shape, 1)
        tq = lax.broadcasted_iota(jnp.int32, sn.shape, 0) % T
        sn = jnp.where(tk <= tq, sn + slope * tk.astype(F32), -jnp.inf)
        _flash_update(sn, vn_ref[...], m_sc, l_sc, acc_sc)
        lam = _lambda(lamv_ref, lam_init)
        for h in range(heads):
            blk = acc_sc[h * 2 * T:(h + 1) * 2 * T, h * vd:(h + 1) * vd] / l_sc[h * 2 * T:(h + 1) * 2 * T, :]
            o = blk[:T, :] - lam * blk[T:, :]
            o = o * lax.rsqrt(jnp.mean(o * o, axis=1, keepdims=True) + LN_EPS) * g_ref[...] * (1.0 - lam_init)
            o_ref[:, h * vd:(h + 1) * vd] = o.astype(o_ref.dtype)


def _attn_decode(q, kb_new, vb_new, cache_k, cache_v, page_table, layer, slopes, lamv, norm_g,
                 batch, T, heads, vd, lam_init):
    depth, n_phys, page = cache_k.shape[:3]
    n_pages = page_table.shape[1]
    past = n_pages * page
    pages = math.gcd(PAGES_PER_STEP, n_pages)
    aw = heads * vd
    qd = vd // 2
    rows = heads * 2 * T
    assert 2 * T == SUBLANES
    q3 = q.reshape(batch, 1, T, aw)
    colmap = jnp.arange(aw, dtype=jnp.int32) // qd
    rowmap = jnp.arange(heads * 2, dtype=jnp.int32)
    q2 = jnp.where((colmap[None, :] == rowmap[:, None])[None, :, None, :], q3, jnp.zeros_like(q3))
    q2 = q2.reshape(batch, rows, aw)
    slope_rows = jnp.repeat(slopes, 2 * T).reshape(rows, 1)
    pad = lambda a: jnp.pad(a.reshape(batch, T, aw), ((0, 0), (0, SUBLANES - T), (0, 0)))
    ck = cache_k.reshape(depth * n_phys, page, aw)
    cv = cache_v.reshape(depth * n_phys, page, aw)
    page_spec = lambda i: pl.BlockSpec(
        (None, page, aw), lambda b, c, pt: (layer * n_phys + pt[b * n_pages + c * pages + i], 0, 0))
    const = lambda a: pl.BlockSpec(a.shape, lambda b, c, pt: (0,) * a.ndim)
    per_b = lambda r: pl.BlockSpec((None, r, aw), lambda b, c, pt: (b, 0, 0))
    grid_spec = pltpu.PrefetchScalarGridSpec(
        num_scalar_prefetch=1,
        grid=(batch, n_pages // pages),
        in_specs=[per_b(rows), per_b(SUBLANES), per_b(SUBLANES), const(slope_rows), const(lamv), const(norm_g)]
                 + [page_spec(i) for i in range(pages)] + [page_spec(i) for i in range(pages)],
        out_specs=per_b(T),
        scratch_shapes=[pltpu.VMEM((rows, 1), F32), pltpu.VMEM((rows, 1), F32), pltpu.VMEM((rows, aw), F32)],
    )
    return pl.pallas_call(
        functools.partial(_attn_decode_kernel, pages=pages, page=page, past=past, heads=heads, vd=vd, T=T,
                          lam_init=lam_init),
        grid_spec=grid_spec,
        out_shape=jax.ShapeDtypeStruct((batch, T, aw), F32),
        compiler_params=_params("parallel", "arbitrary"),
        name="attn_decode",
    )(page_table.reshape(-1), q2, pad(kb_new), pad(vb_new), slope_rows, lamv, norm_g,
      *([ck] * pages), *([cv] * pages))


def _layer_norm(x, g, b):
    mu = jnp.mean(x, axis=1, keepdims=True)
    xc = x - mu
    var = jnp.mean(xc * xc, axis=1, keepdims=True)
    return xc * lax.rsqrt(var + LN_EPS) * g + b


def _post_kernel(hm_ref, ha_ref, x_ref, wom_ref, woa_ref, l1g_ref, l1b_ref, wg_ref, wu_ref, wd_ref,
                 l2g_ref, l2b_ref, y_ref, *, alpha, ff_chunks):
    mix = (jnp.dot(hm_ref[...].astype(BF16), wom_ref[...], preferred_element_type=F32)
           + jnp.dot(ha_ref[...].astype(BF16), woa_ref[...], preferred_element_type=F32))
    x1 = _layer_norm(alpha * x_ref[...] + mix, l1g_ref[...], l1b_ref[...])
    x1b = x1.astype(BF16)
    fc = wg_ref.shape[1] // ff_chunks
    ffn = None
    for f in range(ff_chunks):
        gate = jnp.dot(x1b, wg_ref[:, f * fc:(f + 1) * fc], preferred_element_type=F32)
        up = jnp.dot(x1b, wu_ref[:, f * fc:(f + 1) * fc], preferred_element_type=F32)
        mid = (gate * _sigmoid(gate) * up).astype(BF16)
        part = jnp.dot(mid, wd_ref[f * fc:(f + 1) * fc, :], preferred_element_type=F32)
        ffn = part if ffn is None else ffn + part
    y_ref[...] = _layer_norm(alpha * x1 + ffn, l2g_ref[...], l2b_ref[...])


def _post(hm, ha, x2d, w_out_m, w_out_a, ln1_g, ln1_b, w_gate, w_up, w_down, ln2_g, ln2_b, alpha):
    rows, d = x2d.shape
    tm = min(ROW_TILE, rows)
    assert rows % tm == 0
    d_ff = w_gate.shape[1]
    ff_chunks = 2 if d_ff % (2 * LANES) == 0 else 1
    row = lambda a: pl.BlockSpec((tm, a.shape[1]), lambda i: (i, 0))
    const = lambda a: pl.BlockSpec(a.shape, lambda i: (0,) * a.ndim, pipeline_mode=pl.Buffered(1))
    return pl.pallas_call(
        functools.partial(_post_kernel, alpha=alpha, ff_chunks=ff_chunks),
        grid=(rows // tm,),
        in_specs=[row(hm), row(ha), row(x2d), const(w_out_m), const(w_out_a), const(ln1_g), const(ln1_b),
                  const(w_gate), const(w_up), const(w_down), const(ln2_g), const(ln2_b)],
        out_specs=pl.BlockSpec((tm, d), lambda i: (i, 0)),
        out_shape=jax.ShapeDtypeStruct((rows, d), F32),
        compiler_params=_params("parallel"),
        name="post_ffn",
    )(hm, ha, x2d, w_out_m, w_out_a, ln1_g, ln1_b, w_gate, w_up, w_down, ln2_g, ln2_b)


def kernel(x_prompt, x_sample, cache_k, cache_v, page_table, state_C, state_n, state_m, state_conv, w_in, b_ig, b_fg, conv_w, conv_b, m_norm_g, lam_q1, lam_k1, lam_q2, lam_k2, a_norm_g, w_out, ln1_g, ln1_b, w_gate, w_up, w_down, ln2_g, ln2_b):
    B, S, D = x_prompt.shape
    DB, T, _ = x_sample.shape
    depth = w_in.shape[0]
    heads = b_ig.shape[1]
    mw = m_norm_g.shape[1]
    hd = mw // heads
    vd = a_norm_g.shape[1]
    aw = w_out.shape[1] - mw
    a_heads = aw // vd
    alpha = (2.0 * depth) ** 0.25
    slopes = jnp.exp2(-8.0 * jnp.arange(1, a_heads + 1, dtype=F32) / a_heads)
    q_scale = (vd // 2) ** -0.5

    hp = x_prompt.reshape(B * S, D)
    hs = x_sample.reshape(DB * T, D)
    outs = [[] for _ in range(12)]
    for l in range(depth):
        lam_init = 0.8 - 0.6 * math.exp(-0.3 * l)
        w = w_in[l]
        g_lo = 4 * mw
        g_hi = g_lo + 2 * heads
        w_main = jnp.concatenate([w[:, :g_lo], w[:, g_hi:]], axis=1).astype(BF16)
        w_gates = jnp.pad(w[:, g_lo:g_hi], ((0, 0), (0, LANES - 2 * heads))).astype(BF16)
        gates_b = jnp.concatenate([b_ig[l], b_fg[l]]).astype(F32)
        gb_row = jnp.pad(gates_b, (0, LANES - 2 * heads)).reshape(1, LANES)
        gb_col = jnp.pad(gates_b, (0, SUBLANES - 2 * heads)).reshape(SUBLANES, 1)
        cw = conv_w[l].astype(F32)
        cb = conv_b[l].astype(F32).reshape(1, -1)
        ng = m_norm_g[l].astype(F32).reshape(1, mw)
        ag = a_norm_g[l].astype(F32).reshape(1, vd)
        lamv = jnp.stack([lam_q1[l], lam_k1[l], lam_q2[l], lam_k2[l]]).astype(F32)
        wo = w_out[l].astype(BF16)
        post_w = (wo[:mw], wo[mw:], ln1_g[l].reshape(1, D), ln1_b[l].reshape(1, D),
                  w_gate[l].astype(BF16), w_up[l].astype(BF16), w_down[l].astype(BF16),
                  ln2_g[l].reshape(1, D), ln2_b[l].reshape(1, D))

        qk, vm, om, g, gt, qa, ka, va, kb, vb = _project(hp, w_main, w_gates, mw, aw, q_scale)
        hm, Cp, n_p, mp, tail = _mlstm_prompt(qk, vm, om, g, gt, cw, cb, gb_row, gb_col, ng, B, S, heads, hd)
        ha = _attn_prompt(qa, kb, vb, slopes, lamv, ag, B, S, a_heads, vd, lam_init)
        hp = _post(hm.reshape(B * S, mw), ha.reshape(B * S, aw), hp, *post_w, alpha)
        hist = cw.shape[0] - 1
        outs[0].append(ka.reshape(B, S, a_heads, vd))
        outs[1].append(va.reshape(B, S, a_heads, vd))
        outs[4].append(Cp)
        outs[5].append(n_p)
        outs[6].append(mp[:, :, 0])
        outs[7].append(tail[:, SUBLANES - hist:, :])

        qk, vm, om, g, gt, qa, ka, va, kb, vb = _project(hs, w_main, w_gates, mw, aw, q_scale)
        hm, Cs, n_s, ms = _mlstm_sample(qk, vm, om, g, gt, state_conv, state_C, state_n, state_m, l,
                                        cw, cb, gb_row, gb_col, ng, DB, T, heads, hd)
        ha = _attn_decode(qa, kb, vb, cache_k, cache_v, page_table, l, slopes, lamv, ag,
                          DB, T, a_heads, vd, lam_init)
        hs = _post(hm.reshape(DB * T, mw), ha.reshape(DB * T, aw), hs, *post_w, alpha)
        outs[2].append(ka.reshape(DB, T, a_heads, vd))
        outs[3].append(va.reshape(DB, T, a_heads, vd))
        outs[8].append(Cs)
        outs[9].append(n_s)
        outs[10].append(ms[:, :, 0])
        outs[11].append(qk.reshape(DB, T, 2 * mw)[:, T - hist:, :])

    stack = lambda o: o[0][None] if depth == 1 else jnp.stack(o)
    return (hp.reshape(B, S, D), hs.reshape(DB, T, D), *[stack(o) for o in outs])
```

```python
import functools
import math

import jax
import jax.numpy as jnp
from jax import lax
from jax.experimental import pallas as pl
from jax.experimental.pallas import tpu as pltpu

F32 = jnp.float32
BF16 = jnp.bfloat16

LN_EPS = 1e-5
M_CHUNK = 128
LANES = 128
SUBLANES = 8
ROW_TILE = 512
ATTN_TILE = 1024
ROW_CHUNKS = 8
PAGES_PER_STEP = 16
VMEM_LIMIT = 56 * 1024 * 1024

_NT = (((1,), (1,)), ((), ()))
_TN = (((0,), (0,)), ((), ()))


def _sigmoid(x):
    return 1.0 / (1.0 + jnp.exp(-x))


def _log_sigmoid(x):
    return jnp.minimum(x, 0.0) - jnp.log(1.0 + jnp.exp(-jnp.abs(x)))


def _params(*sem):
    return pltpu.CompilerParams(dimension_semantics=sem, vmem_limit_bytes=VMEM_LIMIT)


def _proj_kernel(x_ref, wm_ref, wg_ref, qk_ref, vm_ref, om_ref, g_ref, gt_ref,
                 q_ref, k_ref, v_ref, kb_ref, vb_ref, *, mw, aw, a_heads, q_scale):
    xb = x_ref[...].astype(BF16)
    tm = xb.shape[0]
    vd = aw // a_heads

    def seg(lo, n):
        return jnp.dot(xb, wm_ref[:, lo:lo + n], preferred_element_type=F32)

    def rows_by_head(ref, a):
        for h in range(a_heads):
            ref[pl.ds(h, tm, stride=a_heads), :] = a[:, h * vd:(h + 1) * vd]

    qk_ref[...] = seg(0, 2 * mw)
    vm_ref[...] = seg(2 * mw, mw).astype(BF16)
    om_ref[...] = seg(3 * mw, mw).astype(BF16)
    q_ref[...] = (seg(4 * mw, aw) * q_scale).astype(BF16)
    k = seg(4 * mw + aw, aw)
    rows_by_head(k_ref, k)
    kb_ref[...] = k.astype(BF16)
    v = seg(4 * mw + 2 * aw, aw)
    rows_by_head(v_ref, v)
    vb_ref[...] = v.astype(BF16)
    g = jnp.dot(xb, wg_ref[...], preferred_element_type=F32)
    g_ref[...] = g
    gt_ref[...] = g.T[:SUBLANES, :]


def _project(x2d, w_main, w_gate, mw, aw, a_heads, q_scale):
    rows, d = x2d.shape
    tm = min(ROW_TILE, rows)
    assert rows % tm == 0
    vd = aw // a_heads
    row = lambda n: pl.BlockSpec((tm, n), lambda i: (i, 0))
    by_head = pl.BlockSpec((tm * a_heads, vd), lambda i: (i, 0))
    const = lambda a: pl.BlockSpec(a.shape, lambda i: (0,) * a.ndim, pipeline_mode=pl.Buffered(1))
    out_shape = (
        jax.ShapeDtypeStruct((rows, 2 * mw), F32),
        jax.ShapeDtypeStruct((rows, mw), BF16),
        jax.ShapeDtypeStruct((rows, mw), BF16),
        jax.ShapeDtypeStruct((rows, LANES), F32),
        jax.ShapeDtypeStruct((SUBLANES, rows), F32),
        jax.ShapeDtypeStruct((rows, aw), BF16),
        jax.ShapeDtypeStruct((rows * a_heads, vd), F32),
        jax.ShapeDtypeStruct((rows * a_heads, vd), F32),
        jax.ShapeDtypeStruct((rows, aw), BF16),
        jax.ShapeDtypeStruct((rows, aw), BF16),
    )
    out_specs = (row(2 * mw), row(mw), row(mw), row(LANES),
                 pl.BlockSpec((SUBLANES, tm), lambda i: (0, i)),
                 row(aw), by_head, by_head, row(aw), row(aw))
    return pl.pallas_call(
        functools.partial(_proj_kernel, mw=mw, aw=aw, a_heads=a_heads, q_scale=q_scale),
        grid=(rows // tm,),
        in_specs=[row(d), const(w_main), const(w_gate)],
        out_specs=out_specs,
        out_shape=out_shape,
        compiler_params=_params("parallel"),
        name="in_proj",
    )(x2d, w_main, w_gate)


def _conv_silu(ext_ref, length, cw_ref, cb_ref):
    width = cw_ref.shape[0]
    y = cb_ref[...] + cw_ref[width - 1:width, :] * ext_ref[pl.ds(SUBLANES, length), :]
    for j in range(width - 1):
        y = y + cw_ref[j:j + 1, :] * ext_ref[pl.ds(SUBLANES - (width - 1) + j, length), :]
    return y * _sigmoid(y)


def _mlstm_chunk(q, k, v, ig_col, ig_row, lf_col, lf_row, C, n, m):
    L = q.shape[0]
    row = lax.broadcasted_iota(jnp.int32, (L, L), 0)
    col = lax.broadcasted_iota(jnp.int32, (L, L), 1)
    tril = col <= row
    b_col = jnp.sum(jnp.where(tril, lf_row, 0.0), axis=1, keepdims=True)
    b_row = jnp.sum(jnp.where(row <= col, lf_col, 0.0), axis=0, keepdims=True)
    log_d = jnp.where(tril, b_col - b_row + ig_row, -jnp.inf)
    m_inter = b_col + m
    m_t = jnp.maximum(m_inter, jnp.max(log_d, axis=1, keepdims=True))
    d_w = jnp.exp(log_d - m_t)
    inter = jnp.exp(m_inter - m_t)
    qb, kb, vb = q.astype(BF16), k.astype(BF16), v.astype(BF16)
    s = lax.dot_general(qb, kb, _NT, preferred_element_type=F32) * d_w
    num = (jnp.dot(s.astype(BF16), vb, preferred_element_type=F32)
           + inter * lax.dot_general(qb, C.astype(BF16), _NT, preferred_element_type=F32))
    den = jnp.sum(s, axis=1, keepdims=True) + inter * jnp.sum(q * n, axis=1, keepdims=True)
    h = num / jnp.maximum(jnp.abs(den), jnp.exp(-m_t))
    m_new = m_t[L - 1:L, :]
    b_last = b_col[L - 1:L, :]
    w_end = jnp.exp(b_last - b_col + ig_col - m_new)
    decay = jnp.exp(b_last + m - m_new)
    vw = (w_end * v.astype(F32)).astype(BF16)
    C_new = decay * C + lax.dot_general(vw, kb, _TN, preferred_element_type=F32)
    n_new = decay * n + jnp.sum(w_end * k, axis=0, keepdims=True)
    return h, C_new, n_new, m_new


def _mlstm_heads(y, vm, om, gc, gr, ng_ref, state, hm_ref, C_ref, n_ref, m_ref, *, heads, hd):
    mw = heads * hd
    lf_c = _log_sigmoid(gc)
    lf_r = _log_sigmoid(gr)
    k_scale = hd ** -0.5
    for h in range(heads):
        sl = slice(h * hd, (h + 1) * hd)
        C, n, m = state(h)
        hh, C_new, n_new, m_new = _mlstm_chunk(
            y[:, sl], y[:, mw + h * hd:mw + (h + 1) * hd] * k_scale, vm[:, sl],
            gc[:, h:h + 1], gr[h:h + 1, :], lf_c[:, heads + h:heads + h + 1], lf_r[heads + h:heads + h + 1, :],
            C, n, m)
        C_ref[h] = C_new
        n_ref[h:h + 1, :] = n_new
        m_ref[h:h + 1, :] = jnp.broadcast_to(m_new, (1, m_ref.shape[1]))
        hn = hh * lax.rsqrt(jnp.mean(hh * hh, axis=1, keepdims=True) + LN_EPS) * ng_ref[:, sl]
        hm_ref[:, sl] = (_sigmoid(om[:, sl].astype(F32)) * hn).astype(hm_ref.dtype)


def _mlstm_prompt_kernel(qk_ref, vm_ref, om_ref, g_ref, gt_ref, cw_ref, cb_ref, gbr_ref, gbc_ref, ng_ref,
                         hm_ref, C_ref, n_ref, m_ref, tail_ref, ext_ref, *, heads, hd):
    c = pl.program_id(1)
    L = qk_ref.shape[0]

    @pl.when(c == 0)
    def _():
        C_ref[...] = jnp.zeros_like(C_ref)
        n_ref[...] = jnp.zeros_like(n_ref)
        m_ref[...] = jnp.zeros_like(m_ref)
        ext_ref[0:SUBLANES, :] = jnp.zeros((SUBLANES, ext_ref.shape[1]), F32)

    u = qk_ref[...]
    ext_ref[pl.ds(SUBLANES, L), :] = u
    y = _conv_silu(ext_ref, L, cw_ref, cb_ref)
    ext_ref[0:SUBLANES, :] = u[L - SUBLANES:L, :]
    tail_ref[...] = u[L - SUBLANES:L, :]
    gc = g_ref[...] + gbr_ref[...]
    gr = gt_ref[...] + gbc_ref[...]
    state = lambda h: (C_ref[h], n_ref[h:h + 1, :], m_ref[h:h + 1, 0:1])
    _mlstm_heads(y, vm_ref[...], om_ref[...], gc, gr, ng_ref, state, hm_ref, C_ref, n_ref, m_ref,
                 heads=heads, hd=hd)


def _mlstm_prompt(qk, vm, om, g, gt, conv_w, conv_b, gb_row, gb_col, norm_g, batch, seq, heads, hd):
    mw = heads * hd
    L = M_CHUNK if seq % M_CHUNK == 0 else seq
    nc = seq // L
    assert L % SUBLANES == 0 and 2 * heads <= SUBLANES and conv_w.shape[0] - 1 <= SUBLANES
    const = lambda a: pl.BlockSpec(a.shape, lambda b, c: (0,) * a.ndim)
    seq_spec = lambda n: pl.BlockSpec((None, L, n), lambda b, c: (b, c, 0))
    out_shape = (
        jax.ShapeDtypeStruct((batch, seq, mw), BF16),
        jax.ShapeDtypeStruct((batch, heads, hd, hd), F32),
        jax.ShapeDtypeStruct((batch, heads, hd), F32),
        jax.ShapeDtypeStruct((batch, heads, LANES), F32),
        jax.ShapeDtypeStruct((batch, SUBLANES, 2 * mw), F32),
    )
    out_specs = (
        seq_spec(mw),
        pl.BlockSpec((None, heads, hd, hd), lambda b, c: (b, 0, 0, 0)),
        pl.BlockSpec((None, heads, hd), lambda b, c: (b, 0, 0)),
        pl.BlockSpec((None, heads, LANES), lambda b, c: (b, 0, 0)),
        pl.BlockSpec((None, SUBLANES, 2 * mw), lambda b, c: (b, 0, 0)),
    )
    return pl.pallas_call(
        functools.partial(_mlstm_prompt_kernel, heads=heads, hd=hd),
        grid=(batch, nc),
        in_specs=[seq_spec(2 * mw), seq_spec(mw), seq_spec(mw), seq_spec(LANES),
                  pl.BlockSpec((SUBLANES, L), lambda b, c: (0, b * nc + c)),
                  const(conv_w), const(conv_b), const(gb_row), const(gb_col), const(norm_g)],
        out_specs=out_specs,
        out_shape=out_shape,
        scratch_shapes=[pltpu.VMEM((L + SUBLANES, 2 * mw), F32)],
        compiler_params=_params("parallel", "arbitrary"),
        name="mlstm_prompt",
    )(qk.reshape(batch, seq, 2 * mw), vm.reshape(batch, seq, mw), om.reshape(batch, seq, mw),
      g.reshape(batch, seq, LANES), gt, conv_w, conv_b, gb_row, gb_col, norm_g)


def _mlstm_sample_kernel(qk_ref, vm_ref, om_ref, g_ref, gt_ref, conv_ref, C0_ref, n0_ref, m0_ref,
                         cw_ref, cb_ref, gbr_ref, gbc_ref, ng_ref,
                         hm_ref, C_ref, n_ref, m_ref, ext_ref, *, heads, hd):
    T = qk_ref.shape[0]
    hist = conv_ref.shape[0]
    ext_ref[pl.ds(SUBLANES - hist, hist), :] = conv_ref[...]
    ext_ref[pl.ds(SUBLANES, T), :] = qk_ref[...]
    y = _conv_silu(ext_ref, T, cw_ref, cb_ref)
    gc = g_ref[...] + gbr_ref[...]
    gr = gt_ref[...] + gbc_ref[...]
    state = lambda h: (C0_ref[h], n0_ref[h:h + 1, :], m0_ref[:, h:h + 1])
    _mlstm_heads(y, vm_ref[...], om_ref[...], gc, gr, ng_ref, state, hm_ref, C_ref, n_ref, m_ref,
                 heads=heads, hd=hd)


def _mlstm_sample(qk, vm, om, g, gt, state_conv, state_C, state_n, state_m, layer,
                  conv_w, conv_b, gb_row, gb_col, norm_g, batch, T, heads, hd):
    mw = heads * hd
    hist = state_conv.shape[-2]
    assert T <= M_CHUNK and hist == conv_w.shape[0] - 1 and hist <= SUBLANES and T <= SUBLANES
    gt3 = gt.reshape(SUBLANES, batch, T).transpose(1, 0, 2)
    const = lambda a: pl.BlockSpec(a.shape, lambda b: (0,) * a.ndim)
    tok = lambda n: pl.BlockSpec((None, T, n), lambda b: (b, 0, 0))
    lb = lambda b: layer * batch + b
    out_shape = (
        jax.ShapeDtypeStruct((batch, T, mw), F32),
        jax.ShapeDtypeStruct((batch, heads, hd, hd), F32),
        jax.ShapeDtypeStruct((batch, heads, hd), F32),
        jax.ShapeDtypeStruct((batch, heads, LANES), F32),
    )
    out_specs = (
        tok(mw),
        pl.BlockSpec((None, heads, hd, hd), lambda b: (b, 0, 0, 0)),
        pl.BlockSpec((None, heads, hd), lambda b: (b, 0, 0)),
        pl.BlockSpec((None, heads, LANES), lambda b: (b, 0, 0)),
    )
    depth = state_C.shape[0]
    return pl.pallas_call(
        functools.partial(_mlstm_sample_kernel, heads=heads, hd=hd),
        grid=(batch,),
        in_specs=[tok(2 * mw), tok(mw), tok(mw), tok(LANES),
                  pl.BlockSpec((None, SUBLANES, T), lambda b: (b, 0, 0)),
                  pl.BlockSpec((None, hist, 2 * mw), lambda b: (lb(b), 0, 0)),
                  pl.BlockSpec((None, heads, hd, hd), lambda b: (lb(b), 0, 0, 0)),
                  pl.BlockSpec((None, heads, hd), lambda b: (lb(b), 0, 0)),
                  pl.BlockSpec((None, 1, heads), lambda b: (lb(b), 0, 0)),
                  const(conv_w), const(conv_b), const(gb_row), const(gb_col), const(norm_g)],
        out_specs=out_specs,
        out_shape=out_shape,
        scratch_shapes=[pltpu.VMEM((2 * SUBLANES, 2 * mw), F32)],
        compiler_params=_params("parallel"),
        name="mlstm_sample",
    )(qk.reshape(batch, T, 2 * mw), vm.reshape(batch, T, mw), om.reshape(batch, T, mw),
      g.reshape(batch, T, LANES), gt3,
      state_conv.reshape(depth * batch, hist, 2 * mw), state_C.reshape(depth * batch, heads, hd, hd),
      state_n.reshape(depth * batch, heads, hd), state_m.reshape(depth * batch, 1, heads),
      conv_w, conv_b, gb_row, gb_col, norm_g)


def _lambda(lamv_ref, lam_init):
    a = jnp.sum(lamv_ref[0:1, :] * lamv_ref[1:2, :], axis=1, keepdims=True)
    b = jnp.sum(lamv_ref[2:3, :] * lamv_ref[3:4, :], axis=1, keepdims=True)
    return jnp.exp(a) - jnp.exp(b) + lam_init


def _flash_update(s, v, m_sc, l_sc, acc_sc):
    m_prev = m_sc[...]
    m_new = jnp.maximum(m_prev, jnp.max(s, axis=1, keepdims=True))
    alpha = jnp.exp(m_prev - m_new)
    p = jnp.exp(s - m_new)
    l_sc[...] = alpha * l_sc[...] + jnp.sum(p, axis=1, keepdims=True)
    acc_sc[...] = alpha * acc_sc[...] + jnp.dot(p.astype(BF16), v, preferred_element_type=F32)
    m_sc[...] = m_new


def _attn_prompt_kernel(slopes_ref, q_ref, k_ref, v_ref, lamv_ref, g_ref, o_ref, va_sc, m_sc, acc_sc,
                        *, tile, qd, lam_init):
    h = pl.program_id(1)
    i = pl.program_id(2)
    vd = v_ref.shape[1]

    @pl.when(i == 0)
    def _():
        va_sc[:, :vd] = v_ref[...]
        va_sc[:, vd:] = jnp.ones((va_sc.shape[0], va_sc.shape[1] - vd), va_sc.dtype)

    slope = slopes_ref[h]
    q = q_ref[...]
    lane = lax.broadcasted_iota(jnp.int32, q.shape, 1)
    zero = jnp.zeros_like(q)
    qs = jnp.concatenate([jnp.where(lane < qd, q, zero), jnp.where(lane >= qd, q, zero)], axis=0)
    m_sc[...] = jnp.full_like(m_sc, -jnp.inf)
    acc_sc[...] = jnp.zeros_like(acc_sc)
    colf = lax.broadcasted_iota(jnp.int32, (1, tile), 1).astype(F32)
    rc = 2 * tile // ROW_CHUNKS

    def step(j, masked):
        start = pl.multiple_of(j * tile, tile)
        bias = slope * (colf + ((j - i) * tile).astype(F32))
        for c in range(ROW_CHUNKS):
            rows = pl.ds(c * rc, rc)
            r0 = (c * rc) % tile
            kw = r0 + rc if masked else tile
            k = k_ref[pl.ds(start, kw), :]
            va = va_sc[pl.ds(start, kw), :]
            s = lax.dot_general(qs[c * rc:(c + 1) * rc, :], k, _NT, preferred_element_type=F32) + bias[:, :kw]
            if masked:
                r = lax.broadcasted_iota(jnp.int32, s.shape, 0) + r0
                cc = lax.broadcasted_iota(jnp.int32, s.shape, 1)
                s = jnp.where(cc <= r, s, -jnp.inf)
            m_prev = m_sc[rows, :]
            m_new = jnp.maximum(m_prev, jnp.max(s, axis=1, keepdims=True))
            alpha = jnp.exp(m_prev - m_new)
            p = jnp.exp(s - jnp.concatenate([m_new] * (kw // LANES), axis=1))
            pv = jnp.dot(p.astype(BF16), va, preferred_element_type=F32)
            acc_sc[rows, :] = jnp.concatenate([alpha] * (acc_sc.shape[1] // LANES), axis=1) * acc_sc[rows, :] + pv
            m_sc[rows, :] = m_new

    def body(j, carry):
        step(j, False)
        return carry

    lax.fori_loop(0, i, body, 0)
    step(i, True)

    lam = _lambda(lamv_ref, lam_init)
    o = acc_sc[:, :vd] / acc_sc[:, vd:]
    o = o[:tile, :] - lam * o[tile:, :]
    o = o * lax.rsqrt(jnp.mean(o * o, axis=1, keepdims=True) + LN_EPS) * g_ref[...] * (1.0 - lam_init)
    o_ref[...] = o.astype(o_ref.dtype)


def _attn_prompt(q, kb, vb, slopes, lamv, norm_g, batch, seq, heads, vd, lam_init):
    tile = min(ATTN_TILE, seq)
    assert seq % tile == 0 and vd == LANES and tile % (2 * tile // ROW_CHUNKS) == 0
    aw = heads * vd
    const = lambda a: pl.BlockSpec(a.shape, lambda b, h, i: (0,) * a.ndim)
    return pl.pallas_call(
        functools.partial(_attn_prompt_kernel, tile=tile, qd=vd // 2, lam_init=lam_init),
        grid=(batch, heads, seq // tile),
        in_specs=[pl.BlockSpec(memory_space=pltpu.SMEM),
                  pl.BlockSpec((None, tile, vd), lambda b, h, i: (b, i, h)),
                  pl.BlockSpec((None, seq, vd), lambda b, h, i: (b, 0, h)),
                  pl.BlockSpec((None, seq, vd), lambda b, h, i: (b, 0, h)),
                  const(lamv), const(norm_g)],
        out_specs=pl.BlockSpec((None, tile, vd), lambda b, h, i: (b, i, h)),
        out_shape=jax.ShapeDtypeStruct((batch, seq, aw), BF16),
        scratch_shapes=[pltpu.VMEM((seq, 2 * vd), BF16), pltpu.VMEM((2 * tile, LANES), F32),
                        pltpu.VMEM((2 * tile, 2 * vd), F32)],
        compiler_params=_params("parallel", "parallel", "arbitrary"),
        name="attn_prompt",
    )(slopes, q.reshape(batch, seq, aw), kb.reshape(batch, seq, aw), vb.reshape(batch, seq, aw), lamv, norm_g)


def _attn_decode_kernel(pt_ref, q2_ref, kn_ref, vn_ref, slope_ref, lamv_ref, g_ref, *rest,
                        pages, page, past, heads, vd, T, lam_init):
    k_refs = rest[:pages]
    v_refs = rest[pages:2 * pages]
    o_ref, m_sc, l_sc, acc_sc = rest[2 * pages:]
    c = pl.program_id(1)
    nk = pages * page

    @pl.when(c == 0)
    def _():
        m_sc[...] = jnp.full_like(m_sc, -jnp.inf)
        l_sc[...] = jnp.zeros_like(l_sc)
        acc_sc[...] = jnp.zeros_like(acc_sc)

    q2 = q2_ref[...]
    slope = slope_ref[...]

    def page_rows(r):
        return jnp.concatenate([r[pl.ds(h, page, stride=heads), :] for h in range(heads)], axis=1).astype(BF16)

    kc = jnp.concatenate([page_rows(r) for r in k_refs], axis=0)
    vc = jnp.concatenate([page_rows(r) for r in v_refs], axis=0)
    s = lax.dot_general(q2, kc, _NT, preferred_element_type=F32)
    pos = (lax.broadcasted_iota(jnp.int32, (1, nk), 1) + (c * nk - past)).astype(F32)
    _flash_update(s + slope * pos, vc, m_sc, l_sc, acc_sc)

    @pl.when(c == pl.num_programs(1) - 1)
    def _():
        rows = q2.shape[0]
        kn = kn_ref[...]
        sn = lax.dot_general(q2, kn, _NT, preferred_element_type=F32)
        tk = lax.broadcasted_iota(jnp.int32, sn.shape, 1)
        tq = lax.broadcasted_iota(jnp.int32, sn.shape, 0) % T
        sn = jnp.where(tk <= tq, sn + slope * tk.astype(F32), -jnp.inf)
        _flash_update(sn, vn_ref[...], m_sc, l_sc, acc_sc)
        lam = _lambda(lamv_ref, lam_init)
        for h in range(heads):
            blk = acc_sc[h * 2 * T:(h + 1) * 2 * T, h * vd:(h + 1) * vd] / l_sc[h * 2 * T:(h + 1) * 2 * T, :]
            o = blk[:T, :] - lam * blk[T:, :]
            o = o * lax.rsqrt(jnp.mean(o * o, axis=1, keepdims=True) + LN_EPS) * g_ref[...] * (1.0 - lam_init)
            o_ref[:, h * vd:(h + 1) * vd] = o.astype(o_ref.dtype)


def _attn_decode(q, kb_new, vb_new, cache_k, cache_v, page_table, layer, slopes, lamv, norm_g,
                 batch, T, heads, vd, lam_init):
    depth, n_phys, page = cache_k.shape[:3]
    n_pages = page_table.shape[1]
    past = n_pages * page
    pages = math.gcd(PAGES_PER_STEP, n_pages)
    aw = heads * vd
    qd = vd // 2
    rows = heads * 2 * T
    assert 2 * T == SUBLANES
    q3 = q.reshape(batch, 1, T, aw)
    colmap = jnp.arange(aw, dtype=jnp.int32) // qd
    rowmap = jnp.arange(heads * 2, dtype=jnp.int32)
    q2 = jnp.where((colmap[None, :] == rowmap[:, None])[None, :, None, :], q3, jnp.zeros_like(q3))
    q2 = q2.reshape(batch, rows, aw)
    slope_rows = jnp.repeat(slopes, 2 * T).reshape(rows, 1)
    pad = lambda a: jnp.pad(a.reshape(batch, T, aw), ((0, 0), (0, SUBLANES - T), (0, 0)))
    ck = cache_k.reshape(depth * n_phys * page * heads, vd)
    cv = cache_v.reshape(depth * n_phys * page * heads, vd)
    page_spec = lambda i: pl.BlockSpec(
        (page * heads, vd), lambda b, c, pt: (layer * n_phys + pt[b * n_pages + c * pages + i], 0))
    const = lambda a: pl.BlockSpec(a.shape, lambda b, c, pt: (0,) * a.ndim)
    per_b = lambda r: pl.BlockSpec((None, r, aw), lambda b, c, pt: (b, 0, 0))
    grid_spec = pltpu.PrefetchScalarGridSpec(
        num_scalar_prefetch=1,
        grid=(batch, n_pages // pages),
        in_specs=[per_b(rows), per_b(SUBLANES), per_b(SUBLANES), const(slope_rows), const(lamv), const(norm_g)]
                 + [page_spec(i) for i in range(pages)] + [page_spec(i) for i in range(pages)],
        out_specs=per_b(T),
        scratch_shapes=[pltpu.VMEM((rows, 1), F32), pltpu.VMEM((rows, 1), F32), pltpu.VMEM((rows, aw), F32)],
    )
    return pl.pallas_call(
        functools.partial(_attn_decode_kernel, pages=pages, page=page, past=past, heads=heads, vd=vd, T=T,
                          lam_init=lam_init),
        grid_spec=grid_spec,
        out_shape=jax.ShapeDtypeStruct((batch, T, aw), F32),
        compiler_params=_params("parallel", "arbitrary"),
        name="attn_decode",
    )(page_table.reshape(-1), q2, pad(kb_new), pad(vb_new), slope_rows, lamv, norm_g,
      *([ck] * pages), *([cv] * pages))


def _layer_norm(x, g, b):
    mu = jnp.mean(x, axis=1, keepdims=True)
    xc = x - mu
    var = jnp.mean(xc * xc, axis=1, keepdims=True)
    return xc * lax.rsqrt(var + LN_EPS) * g + b


def _post_kernel(hm_ref, ha_ref, x_ref, wom_ref, woa_ref, l1g_ref, l1b_ref, wg_ref, wu_ref, wd_ref,
                 l2g_ref, l2b_ref, y_ref, *, alpha, ff_chunks):
    mix = (jnp.dot(hm_ref[...].astype(BF16), wom_ref[...], preferred_element_type=F32)
           + jnp.dot(ha_ref[...].astype(BF16), woa_ref[...], preferred_element_type=F32))
    x1 = _layer_norm(alpha * x_ref[...] + mix, l1g_ref[...], l1b_ref[...])
    x1b = x1.astype(BF16)
    fc = wg_ref.shape[1] // ff_chunks
    ffn = None
    for f in range(ff_chunks):
        gate = jnp.dot(x1b, wg_ref[:, f * fc:(f + 1) * fc], preferred_element_type=F32)
        up = jnp.dot(x1b, wu_ref[:, f * fc:(f + 1) * fc], preferred_element_type=F32)
        mid = (gate * _sigmoid(gate) * up).astype(BF16)
        part = jnp.dot(mid, wd_ref[f * fc:(f + 1) * fc, :], preferred_element_type=F32)
        ffn = part if ffn is None else ffn + part
    y_ref[...] = _layer_norm(alpha * x1 + ffn, l2g_ref[...], l2b_ref[...])


def _post(hm, ha, x2d, w_out_m, w_out_a, ln1_g, ln1_b, w_gate, w_up, w_down, ln2_g, ln2_b, alpha):
    rows, d = x2d.shape
    tm = min(ROW_TILE, rows)
    assert rows % tm == 0
    d_ff = w_gate.shape[1]
    ff_chunks = 2 if d_ff % (2 * LANES) == 0 else 1
    row = lambda a: pl.BlockSpec((tm, a.shape[1]), lambda i: (i, 0))
    const = lambda a: pl.BlockSpec(a.shape, lambda i: (0,) * a.ndim, pipeline_mode=pl.Buffered(1))
    return pl.pallas_call(
        functools.partial(_post_kernel, alpha=alpha, ff_chunks=ff_chunks),
        grid=(rows // tm,),
        in_specs=[row(hm), row(ha), row(x2d), const(w_out_m), const(w_out_a), const(ln1_g), const(ln1_b),
                  const(w_gate), const(w_up), const(w_down), const(ln2_g), const(ln2_b)],
        out_specs=pl.BlockSpec((tm, d), lambda i: (i, 0)),
        out_shape=jax.ShapeDtypeStruct((rows, d), F32),
        compiler_params=_params("parallel"),
        name="post_ffn",
    )(hm, ha, x2d, w_out_m, w_out_a, ln1_g, ln1_b, w_gate, w_up, w_down, ln2_g, ln2_b)


def kernel(x_prompt, x_sample, cache_k, cache_v, page_table, state_C, state_n, state_m, state_conv, w_in, b_ig, b_fg, conv_w, conv_b, m_norm_g, lam_q1, lam_k1, lam_q2, lam_k2, a_norm_g, w_out, ln1_g, ln1_b, w_gate, w_up, w_down, ln2_g, ln2_b):
    B, S, D = x_prompt.shape
    DB, T, _ = x_sample.shape
    depth = w_in.shape[0]
    heads = b_ig.shape[1]
    mw = m_norm_g.shape[1]
    hd = mw // heads
    vd = a_norm_g.shape[1]
    aw = w_out.shape[1] - mw
    a_heads = aw // vd
    alpha = (2.0 * depth) ** 0.25
    slopes = jnp.exp2(-8.0 * jnp.arange(1, a_heads + 1, dtype=F32) / a_heads)
    q_scale = (vd // 2) ** -0.5

    hp = x_prompt.reshape(B * S, D)
    hs = x_sample.reshape(DB * T, D)
    outs = [[] for _ in range(12)]
    for l in range(depth):
        lam_init = 0.8 - 0.6 * math.exp(-0.3 * l)
        w = w_in[l]
        g_lo = 4 * mw
        g_hi = g_lo + 2 * heads
        w_main = jnp.concatenate([w[:, :g_lo], w[:, g_hi:]], axis=1).astype(BF16)
        w_gates = jnp.pad(w[:, g_lo:g_hi], ((0, 0), (0, LANES - 2 * heads))).astype(BF16)
        gates_b = jnp.concatenate([b_ig[l], b_fg[l]]).astype(F32)
        gb_row = jnp.pad(gates_b, (0, LANES - 2 * heads)).reshape(1, LANES)
        gb_col = jnp.pad(gates_b, (0, SUBLANES - 2 * heads)).reshape(SUBLANES, 1)
        cw = conv_w[l].astype(F32)
        cb = conv_b[l].astype(F32).reshape(1, -1)
        ng = m_norm_g[l].astype(F32).reshape(1, mw)
        ag = a_norm_g[l].astype(F32).reshape(1, vd)
        lamv = jnp.stack([lam_q1[l], lam_k1[l], lam_q2[l], lam_k2[l]]).astype(F32)
        wo = w_out[l].astype(BF16)
        post_w = (wo[:mw], wo[mw:], ln1_g[l].reshape(1, D), ln1_b[l].reshape(1, D),
                  w_gate[l].astype(BF16), w_up[l].astype(BF16), w_down[l].astype(BF16),
                  ln2_g[l].reshape(1, D), ln2_b[l].reshape(1, D))

        qk, vm, om, g, gt, qa, ka, va, kb, vb = _project(hp, w_main, w_gates, mw, aw, a_heads, q_scale)
        hm, Cp, n_p, mp, tail = _mlstm_prompt(qk, vm, om, g, gt, cw, cb, gb_row, gb_col, ng, B, S, heads, hd)
        ha = _attn_prompt(qa, kb, vb, slopes, lamv, ag, B, S, a_heads, vd, lam_init)
        hp = _post(hm.reshape(B * S, mw), ha.reshape(B * S, aw), hp, *post_w, alpha)
        hist = cw.shape[0] - 1
        outs[0].append(ka.reshape(B, S, a_heads, vd))
        outs[1].append(va.reshape(B, S, a_heads, vd))
        outs[4].append(Cp)
        outs[5].append(n_p)
        outs[6].append(mp[:, :, 0])
        outs[7].append(tail[:, SUBLANES - hist:, :])

        qk, vm, om, g, gt, qa, ka, va, kb, vb = _project(hs, w_main, w_gates, mw, aw, a_heads, q_scale)
        hm, Cs, n_s, ms = _mlstm_sample(qk, vm, om, g, gt, state_conv, state_C, state_n, state_m, l,
                                        cw, cb, gb_row, gb_col, ng, DB, T, heads, hd)
        ha = _attn_decode(qa, kb, vb, cache_k, cache_v, page_table, l, slopes, lamv, ag,
                          DB, T, a_heads, vd, lam_init)
        hs = _post(hm.reshape(DB * T, mw), ha.reshape(DB * T, aw), hs, *post_w, alpha)
        outs[2].append(ka.reshape(DB, T, a_heads, vd))
        outs[3].append(va.reshape(DB, T, a_heads, vd))
        outs[8].append(Cs)
        outs[9].append(n_s)
        outs[10].append(ms[:, :, 0])
        outs[11].append(qk.reshape(DB, T, 2 * mw)[:, T - hist:, :])

    stack = lambda o: o[0][None] if depth == 1 else jnp.stack(o)
    return (hp.reshape(B, S, D), hs.reshape(DB, T, D), *[stack(o) for o in outs])
```

```python
import functools
import math

import jax
import jax.numpy as jnp
from jax import lax
from jax.experimental import pallas as pl
from jax.experimental.pallas import tpu as pltpu

F32 = jnp.float32
BF16 = jnp.bfloat16

LN_EPS = 1e-5
M_CHUNK = 128
LANES = 128
SUBLANES = 8
ROW_TILE = 512
ATTN_TILE = 1024
MLSTM_GROUP = 2
ROW_CHUNKS = 8
PAGES_PER_STEP = 16
VMEM_LIMIT = 56 * 1024 * 1024

_NT = (((1,), (1,)), ((), ()))
_TN = (((0,), (0,)), ((), ()))


def _sigmoid(x):
    return 1.0 / (1.0 + jnp.exp(-x))


def _log_sigmoid(x):
    return jnp.minimum(x, 0.0) - jnp.log(1.0 + jnp.exp(-jnp.abs(x)))


def _params(*sem):
    return pltpu.CompilerParams(dimension_semantics=sem, vmem_limit_bytes=VMEM_LIMIT)


def _proj_kernel(x_ref, wm_ref, wg_ref, qk_ref, vm_ref, om_ref, g_ref, gt_ref,
                 q_ref, k_ref, v_ref, kb_ref, vb_ref, *, mw, aw, a_heads, q_scale):
    xb = x_ref[...].astype(BF16)
    tm = xb.shape[0]
    vd = aw // a_heads

    def seg(lo, n):
        return jnp.dot(xb, wm_ref[:, lo:lo + n], preferred_element_type=F32)

    def rows_by_head(ref, a):
        for h in range(a_heads):
            ref[pl.ds(h, tm, stride=a_heads), :] = a[:, h * vd:(h + 1) * vd]

    qk_ref[...] = seg(0, 2 * mw)
    vm_ref[...] = seg(2 * mw, mw).astype(BF16)
    om_ref[...] = seg(3 * mw, mw).astype(BF16)
    q_ref[...] = (seg(4 * mw, aw) * q_scale).astype(BF16)
    k = seg(4 * mw + aw, aw)
    rows_by_head(k_ref, k)
    kb_ref[...] = k.astype(BF16)
    v = seg(4 * mw + 2 * aw, aw)
    rows_by_head(v_ref, v)
    vb_ref[...] = v.astype(BF16)
    g = jnp.dot(xb, wg_ref[...], preferred_element_type=F32)
    g_ref[...] = g
    gt_ref[...] = g.T[:SUBLANES, :]


def _project(x2d, w_main, w_gate, mw, aw, a_heads, q_scale, seq):
    rows, d = x2d.shape
    tm = min(ROW_TILE, rows)
    assert rows % seq == 0 and seq % tm == 0
    tiles = seq // tm
    vd = aw // a_heads
    row = lambda n: pl.BlockSpec((tm, n), lambda i: (i, 0))
    by_head = pl.BlockSpec((tm * a_heads, vd), lambda i: (i, 0))
    const = lambda a: pl.BlockSpec(a.shape, lambda i: (0,) * a.ndim, pipeline_mode=pl.Buffered(1))
    out_shape = (
        jax.ShapeDtypeStruct((rows, 2 * mw), F32),
        jax.ShapeDtypeStruct((rows, mw), BF16),
        jax.ShapeDtypeStruct((rows, mw), BF16),
        jax.ShapeDtypeStruct((rows, LANES), F32),
        jax.ShapeDtypeStruct((rows // seq, SUBLANES, seq), F32),
        jax.ShapeDtypeStruct((rows, aw), BF16),
        jax.ShapeDtypeStruct((rows * a_heads, vd), F32),
        jax.ShapeDtypeStruct((rows * a_heads, vd), F32),
        jax.ShapeDtypeStruct((rows, aw), BF16),
        jax.ShapeDtypeStruct((rows, aw), BF16),
    )
    out_specs = (row(2 * mw), row(mw), row(mw), row(LANES),
                 pl.BlockSpec((None, SUBLANES, tm), lambda i: (i // tiles, 0, i % tiles)),
                 row(aw), by_head, by_head, row(aw), row(aw))
    return pl.pallas_call(
        functools.partial(_proj_kernel, mw=mw, aw=aw, a_heads=a_heads, q_scale=q_scale),
        grid=(rows // tm,),
        in_specs=[row(d), const(w_main), const(w_gate)],
        out_specs=out_specs,
        out_shape=out_shape,
        compiler_params=_params("parallel"),
        name="in_proj",
    )(x2d, w_main, w_gate)


def _conv_silu(ext_ref, length, cw_ref, cb_ref):
    width = cw_ref.shape[0]
    y = cb_ref[...] + cw_ref[width - 1:width, :] * ext_ref[pl.ds(SUBLANES, length), :]
    for j in range(width - 1):
        y = y + cw_ref[j:j + 1, :] * ext_ref[pl.ds(SUBLANES - (width - 1) + j, length), :]
    return y * _sigmoid(y)


def _mlstm_chunk(q, k, v, ig_col, ig_row, lf_col, lf_row, C, n, m):
    L = q.shape[0]
    row = lax.broadcasted_iota(jnp.int32, (L, L), 0)
    col = lax.broadcasted_iota(jnp.int32, (L, L), 1)
    tril = col <= row
    b_col = jnp.sum(jnp.where(tril, lf_row, 0.0), axis=1, keepdims=True)
    b_row = jnp.sum(jnp.where(row <= col, lf_col, 0.0), axis=0, keepdims=True)
    log_d = jnp.where(tril, b_col - b_row + ig_row, -jnp.inf)
    m_inter = b_col + m
    m_t = jnp.maximum(m_inter, jnp.max(log_d, axis=1, keepdims=True))
    d_w = jnp.exp(log_d - m_t)
    inter = jnp.exp(m_inter - m_t)
    d = q.shape[1]
    qb, kb, vb = q.astype(BF16), k.astype(BF16), v.astype(BF16)
    qk = lax.dot_general(qb, kb, _NT, preferred_element_type=F32)
    c_n = jnp.concatenate([C, jnp.broadcast_to(n, (d, d))], axis=0).astype(BF16)
    qc = lax.dot_general(qb, c_n, _NT, preferred_element_type=F32)
    yield None
    s = (qk * d_w).astype(BF16)
    v_1 = jnp.concatenate([vb, jnp.ones((L, d), BF16)], axis=1)
    nd = jnp.dot(s, v_1, preferred_element_type=F32) + inter * qc
    yield None
    h = nd[:, :d] / jnp.maximum(jnp.abs(nd[:, d:]), jnp.exp(-m_t))
    m_new = m_t[L - 1:L, :]
    b_last = b_col[L - 1:L, :]
    w_end = jnp.exp(b_last - b_col + ig_col - m_new)
    decay = jnp.exp(b_last + m - m_new)
    vw = (w_end * v.astype(F32)).astype(BF16)
    C_new = decay * C + lax.dot_general(vw, kb, _TN, preferred_element_type=F32)
    n_new = decay * n + jnp.sum(w_end * k, axis=0, keepdims=True)
    yield h, C_new, n_new, m_new


def _mlstm_heads(seqs, ng, *, heads, hd):
    mw = heads * hd
    k_scale = hd ** -0.5
    chains = []
    for y, vm, om, gc, gr, states in seqs:
        lf_c = _log_sigmoid(gc)
        lf_r = _log_sigmoid(gr)
        for h in range(heads):
            sl = slice(h * hd, (h + 1) * hd)
            chains.append(_mlstm_chunk(
                y[:, sl], y[:, mw + h * hd:mw + (h + 1) * hd] * k_scale, vm[:, sl],
                gc[:, h:h + 1], gr[h:h + 1, :], lf_c[:, heads + h:heads + h + 1],
                lf_r[heads + h:heads + h + 1, :], *states[h]))
    for chain in chains:
        next(chain)
    for chain in chains:
        next(chain)
    done = [next(chain) for chain in chains]
    out = []
    for s, (y, vm, om, gc, gr, states) in enumerate(seqs):
        res = []
        for h in range(heads):
            sl = slice(h * hd, (h + 1) * hd)
            hh, C_new, n_new, m_new = done[s * heads + h]
            hn = hh * lax.rsqrt(jnp.mean(hh * hh, axis=1, keepdims=True) + LN_EPS) * ng[:, sl]
            res.append((_sigmoid(om[:, sl].astype(F32)) * hn, C_new, n_new, m_new))
        out.append(res)
    return out


def _mlstm_store(results, hm_ref, C_ref, n_ref, m_ref, hd):
    for s, seq_results in enumerate(results):
        for h, (hm, C_new, n_new, m_new) in enumerate(seq_results):
            hm_ref[s, :, h * hd:(h + 1) * hd] = hm.astype(hm_ref.dtype)
            C_ref[s, h] = C_new
            n_ref[s, h:h + 1, :] = n_new
            m_ref[s, h:h + 1, :] = jnp.broadcast_to(m_new, (1, m_ref.shape[2]))


def _mlstm_prompt_kernel(qk_ref, vm_ref, om_ref, g_ref, gt_ref, cw_ref, cb_ref, gbr_ref, gbc_ref, ng_ref,
                         hm_ref, C_ref, n_ref, m_ref, tail_ref, ext_ref, *, heads, hd):
    c = pl.program_id(1)
    G, L = qk_ref.shape[0], qk_ref.shape[1]

    @pl.when(c == 0)
    def _():
        C_ref[...] = jnp.zeros_like(C_ref)
        n_ref[...] = jnp.zeros_like(n_ref)
        m_ref[...] = jnp.zeros_like(m_ref)
        ext_ref[:, 0:SUBLANES, :] = jnp.zeros((G, SUBLANES, ext_ref.shape[2]), F32)

    states = [[(C_ref[s, h], n_ref[s, h:h + 1, :], m_ref[s, h:h + 1, 0:1]) for h in range(heads)]
              for s in range(G)]
    seqs = []
    for s in range(G):
        u = qk_ref[s]
        ext = ext_ref.at[s]
        ext[pl.ds(SUBLANES, L), :] = u
        y = _conv_silu(ext, L, cw_ref, cb_ref)
        ext[0:SUBLANES, :] = u[L - SUBLANES:L, :]
        tail_ref[s] = u[L - SUBLANES:L, :]
        gc = g_ref[s] + gbr_ref[...]
        gr = gt_ref[s] + gbc_ref[...]
        seqs.append((y, vm_ref[s], om_ref[s], gc, gr, states[s]))
    results = _mlstm_heads(seqs, ng_ref[...], heads=heads, hd=hd)
    _mlstm_store(results, hm_ref, C_ref, n_ref, m_ref, hd)


def _mlstm_prompt(qk, vm, om, g, gt, conv_w, conv_b, gb_row, gb_col, norm_g, batch, seq, heads, hd):
    mw = heads * hd
    L = M_CHUNK if seq % M_CHUNK == 0 else seq
    nc = seq // L
    G = math.gcd(MLSTM_GROUP, batch)
    assert L % SUBLANES == 0 and 2 * heads <= SUBLANES and conv_w.shape[0] - 1 <= SUBLANES
    const = lambda a: pl.BlockSpec(a.shape, lambda b, c: (0,) * a.ndim)
    seq_spec = lambda n: pl.BlockSpec((G, L, n), lambda b, c: (b, c, 0))
    out_shape = (
        jax.ShapeDtypeStruct((batch, seq, mw), BF16),
        jax.ShapeDtypeStruct((batch, heads, hd, hd), F32),
        jax.ShapeDtypeStruct((batch, heads, hd), F32),
        jax.ShapeDtypeStruct((batch, heads, LANES), F32),
        jax.ShapeDtypeStruct((batch, SUBLANES, 2 * mw), F32),
    )
    out_specs = (
        seq_spec(mw),
        pl.BlockSpec((G, heads, hd, hd), lambda b, c: (b, 0, 0, 0)),
        pl.BlockSpec((G, heads, hd), lambda b, c: (b, 0, 0)),
        pl.BlockSpec((G, heads, LANES), lambda b, c: (b, 0, 0)),
        pl.BlockSpec((G, SUBLANES, 2 * mw), lambda b, c: (b, 0, 0)),
    )
    return pl.pallas_call(
        functools.partial(_mlstm_prompt_kernel, heads=heads, hd=hd),
        grid=(batch // G, nc),
        in_specs=[seq_spec(2 * mw), seq_spec(mw), seq_spec(mw), seq_spec(LANES),
                  pl.BlockSpec((G, SUBLANES, L), lambda b, c: (b, 0, c)),
                  const(conv_w), const(conv_b), const(gb_row), const(gb_col), const(norm_g)],
        out_specs=out_specs,
        out_shape=out_shape,
        scratch_shapes=[pltpu.VMEM((G, L + SUBLANES, 2 * mw), F32)],
        compiler_params=_params("parallel", "arbitrary"),
        name="mlstm_prompt",
    )(qk.reshape(batch, seq, 2 * mw), vm.reshape(batch, seq, mw), om.reshape(batch, seq, mw),
      g.reshape(batch, seq, LANES), gt, conv_w, conv_b, gb_row, gb_col, norm_g)


def _mlstm_sample_kernel(qk_ref, vm_ref, om_ref, g_ref, gt_ref, conv_ref, C0_ref, n0_ref, m0_ref,
                         cw_ref, cb_ref, gbr_ref, gbc_ref, ng_ref,
                         hm_ref, C_ref, n_ref, m_ref, ext_ref, *, heads, hd):
    G, T = qk_ref.shape[0], qk_ref.shape[1]
    hist = conv_ref.shape[1]
    seqs = []
    for s in range(G):
        ext = ext_ref.at[s]
        ext[pl.ds(SUBLANES - hist, hist), :] = conv_ref[s]
        ext[pl.ds(SUBLANES, T), :] = qk_ref[s]
        y = _conv_silu(ext, T, cw_ref, cb_ref)
        gc = g_ref[s] + gbr_ref[...]
        gr = gt_ref[s] + gbc_ref[...]
        states = [(C0_ref[s, h], n0_ref[s, h:h + 1, :], m0_ref[s, :, h:h + 1]) for h in range(heads)]
        seqs.append((y, vm_ref[s], om_ref[s], gc, gr, states))
    results = _mlstm_heads(seqs, ng_ref[...], heads=heads, hd=hd)
    _mlstm_store(results, hm_ref, C_ref, n_ref, m_ref, hd)


def _mlstm_sample(qk, vm, om, g, gt, state_conv, state_C, state_n, state_m, layer,
                  conv_w, conv_b, gb_row, gb_col, norm_g, batch, T, heads, hd):
    mw = heads * hd
    hist = state_conv.shape[-2]
    assert T <= M_CHUNK and hist == conv_w.shape[0] - 1 and hist <= SUBLANES and T <= SUBLANES
    gt3 = gt.reshape(SUBLANES, batch, T).transpose(1, 0, 2)
    G = math.gcd(MLSTM_GROUP, batch)
    const = lambda a: pl.BlockSpec(a.shape, lambda b: (0,) * a.ndim)
    tok = lambda n: pl.BlockSpec((G, T, n), lambda b: (b, 0, 0))
    lb = lambda b: layer * (batch // G) + b
    out_shape = (
        jax.ShapeDtypeStruct((batch, T, mw), F32),
        jax.ShapeDtypeStruct((batch, heads, hd, hd), F32),
        jax.ShapeDtypeStruct((batch, heads, hd), F32),
        jax.ShapeDtypeStruct((batch, heads, LANES), F32),
    )
    out_specs = (
        tok(mw),
        pl.BlockSpec((G, heads, hd, hd), lambda b: (b, 0, 0, 0)),
        pl.BlockSpec((G, heads, hd), lambda b: (b, 0, 0)),
        pl.BlockSpec((G, heads, LANES), lambda b: (b, 0, 0)),
    )
    depth = state_C.shape[0]
    return pl.pallas_call(
        functools.partial(_mlstm_sample_kernel, heads=heads, hd=hd),
        grid=(batch // G,),
        in_specs=[tok(2 * mw), tok(mw), tok(mw), tok(LANES),
                  pl.BlockSpec((G, SUBLANES, T), lambda b: (b, 0, 0)),
                  pl.BlockSpec((G, hist, 2 * mw), lambda b: (lb(b), 0, 0)),
                  pl.BlockSpec((G, heads, hd, hd), lambda b: (lb(b), 0, 0, 0)),
                  pl.BlockSpec((G, heads, hd), lambda b: (lb(b), 0, 0)),
                  pl.BlockSpec((G, 1, heads), lambda b: (lb(b), 0, 0)),
                  const(conv_w), const(conv_b), const(gb_row), const(gb_col), const(norm_g)],
        out_specs=out_specs,
        out_shape=out_shape,
        scratch_shapes=[pltpu.VMEM((G, 2 * SUBLANES, 2 * mw), F32)],
        compiler_params=_params("parallel"),
        name="mlstm_sample",
    )(qk.reshape(batch, T, 2 * mw), vm.reshape(batch, T, mw), om.reshape(batch, T, mw),
      g.reshape(batch, T, LANES), gt3,
      state_conv.reshape(depth * batch, hist, 2 * mw), state_C.reshape(depth * batch, heads, hd, hd),
      state_n.reshape(depth * batch, heads, hd), state_m.reshape(depth * batch, 1, heads),
      conv_w, conv_b, gb_row, gb_col, norm_g)


def _lambda(lamv_ref, lam_init):
    a = jnp.sum(lamv_ref[0:1, :] * lamv_ref[1:2, :], axis=1, keepdims=True)
    b = jnp.sum(lamv_ref[2:3, :] * lamv_ref[3:4, :], axis=1, keepdims=True)
    return jnp.exp(a) - jnp.exp(b) + lam_init


def _flash_update(s, v, m_sc, l_sc, acc_sc):
    m_prev = m_sc[...]
    m_new = jnp.maximum(m_prev, jnp.max(s, axis=1, keepdims=True))
    alpha = jnp.exp(m_prev - m_new)
    p = jnp.exp(s - m_new)
    l_sc[...] = alpha * l_sc[...] + jnp.sum(p, axis=1, keepdims=True)
    acc_sc[...] = alpha * acc_sc[...] + jnp.dot(p.astype(BF16), v, preferred_element_type=F32)
    m_sc[...] = m_new


def _attn_prompt_kernel(slopes_ref, q_ref, k_ref, v_ref, lamv_ref, g_ref, o_ref, va_sc, m_sc, acc_sc,
                        *, tile, qd, lam_init):
    h = pl.program_id(1)
    i = pl.program_id(2)
    vd = v_ref.shape[1]

    @pl.when(i == 0)
    def _():
        va_sc[:, :vd] = v_ref[...]
        va_sc[:, vd:] = jnp.ones((va_sc.shape[0], va_sc.shape[1] - vd), va_sc.dtype)

    slope = slopes_ref[h]
    q = q_ref[...]
    lane = lax.broadcasted_iota(jnp.int32, q.shape, 1)
    zero = jnp.zeros_like(q)
    qs = jnp.concatenate([jnp.where(lane < qd, q, zero), jnp.where(lane >= qd, q, zero)], axis=0)
    m_sc[...] = jnp.full_like(m_sc, -jnp.inf)
    acc_sc[...] = jnp.zeros_like(acc_sc)
    colf = lax.broadcasted_iota(jnp.int32, (1, tile), 1).astype(F32)
    rc = 2 * tile // ROW_CHUNKS

    def step(j, masked):
        start = pl.multiple_of(j * tile, tile)
        bias = slope * (colf + ((j - i) * tile).astype(F32))
        for c in range(ROW_CHUNKS):
            rows = pl.ds(c * rc, rc)
            r0 = (c * rc) % tile
            kw = r0 + rc if masked else tile
            k = k_ref[pl.ds(start, kw), :]
            va = va_sc[pl.ds(start, kw), :]
            s = lax.dot_general(qs[c * rc:(c + 1) * rc, :], k, _NT, preferred_element_type=F32) + bias[:, :kw]
            if masked:
                tri = (lax.broadcasted_iota(jnp.int32, (rc, rc), 1) <= lax.broadcasted_iota(jnp.int32, (rc, rc), 0))
                last = jnp.where(tri, s[:, r0:], -jnp.inf)
                s = last if r0 == 0 else jnp.concatenate([s[:, :r0], last], axis=1)
            m_prev = m_sc[rows, :]
            m_new = jnp.maximum(m_prev, jnp.max(s, axis=1, keepdims=True))
            alpha = jnp.exp(m_prev - m_new)
            p = jnp.exp(s - jnp.concatenate([m_new] * (kw // LANES), axis=1))
            pv = jnp.dot(p.astype(BF16), va, preferred_element_type=F32)
            acc_sc[rows, :] = jnp.concatenate([alpha] * (acc_sc.shape[1] // LANES), axis=1) * acc_sc[rows, :] + pv
            m_sc[rows, :] = m_new

    def body(j, carry):
        step(j, False)
        return carry

    lax.fori_loop(0, i, body, 0)
    step(i, True)

    lam = _lambda(lamv_ref, lam_init)
    o = acc_sc[:, :vd] / acc_sc[:, vd:]
    o = o[:tile, :] - lam * o[tile:, :]
    o = o * lax.rsqrt(jnp.mean(o * o, axis=1, keepdims=True) + LN_EPS) * g_ref[...] * (1.0 - lam_init)
    o_ref[...] = o.astype(o_ref.dtype)


def _attn_prompt(q, kb, vb, slopes, lamv, norm_g, batch, seq, heads, vd, lam_init):
    tile = min(ATTN_TILE, seq)
    assert seq % tile == 0 and vd == LANES and tile % (2 * tile // ROW_CHUNKS) == 0
    aw = heads * vd
    const = lambda a: pl.BlockSpec(a.shape, lambda b, h, i: (0,) * a.ndim)
    return pl.pallas_call(
        functools.partial(_attn_prompt_kernel, tile=tile, qd=vd // 2, lam_init=lam_init),
        grid=(batch, heads, seq // tile),
        in_specs=[pl.BlockSpec(memory_space=pltpu.SMEM),
                  pl.BlockSpec((None, tile, vd), lambda b, h, i: (b, i, h)),
                  pl.BlockSpec((None, seq, vd), lambda b, h, i: (b, 0, h)),
                  pl.BlockSpec((None, seq, vd), lambda b, h, i: (b, 0, h)),
                  const(lamv), const(norm_g)],
        out_specs=pl.BlockSpec((None, tile, vd), lambda b, h, i: (b, i, h)),
        out_shape=jax.ShapeDtypeStruct((batch, seq, aw), BF16),
        scratch_shapes=[pltpu.VMEM((seq, 2 * vd), BF16), pltpu.VMEM((2 * tile, LANES), F32),
                        pltpu.VMEM((2 * tile, 2 * vd), F32)],
        compiler_params=_params("parallel", "parallel", "arbitrary"),
        name="attn_prompt",
    )(slopes, q.reshape(batch, seq, aw), kb.reshape(batch, seq, aw), vb.reshape(batch, seq, aw), lamv, norm_g)


def _attn_decode_kernel(pt_ref, q2_ref, kn_ref, vn_ref, slope_ref, lamv_ref, g_ref, *rest,
                        pages, page, past, heads, vd, T, lam_init):
    k_refs = rest[:pages]
    v_refs = rest[pages:2 * pages]
    o_ref, m_sc, l_sc, acc_sc = rest[2 * pages:]
    c = pl.program_id(1)
    nk = pages * page

    @pl.when(c == 0)
    def _():
        m_sc[...] = jnp.full_like(m_sc, -jnp.inf)
        l_sc[...] = jnp.zeros_like(l_sc)
        acc_sc[...] = jnp.zeros_like(acc_sc)

    q2 = q2_ref[...]
    slope = slope_ref[...]

    def page_rows(r):
        return jnp.concatenate([r[pl.ds(h, page, stride=heads), :] for h in range(heads)], axis=1).astype(BF16)

    kc = jnp.concatenate([page_rows(r) for r in k_refs], axis=0)
    vc = jnp.concatenate([page_rows(r) for r in v_refs], axis=0)
    s = lax.dot_general(q2, kc, _NT, preferred_element_type=F32)
    pos = (lax.broadcasted_iota(jnp.int32, (1, nk), 1) + (c * nk - past)).astype(F32)
    _flash_update(s + slope * pos, vc, m_sc, l_sc, acc_sc)

    @pl.when(c == pl.num_programs(1) - 1)
    def _():
        rows = q2.shape[0]
        kn = kn_ref[...]
        sn = lax.dot_general(q2, kn, _NT, preferred_element_type=F32)
        tk = lax.broadcasted_iota(jnp.int32, sn.shape, 1)
        tq = lax.broadcasted_iota(jnp.int32, sn.shape, 0) % T
        sn = jnp.where(tk <= tq, sn + slope * tk.astype(F32), -jnp.inf)
        _flash_update(sn, vn_ref[...], m_sc, l_sc, acc_sc)
        lam = _lambda(lamv_ref, lam_init)
        for h in range(heads):
            blk = acc_sc[h * 2 * T:(h + 1) * 2 * T, h * vd:(h + 1) * vd] / l_sc[h * 2 * T:(h + 1) * 2 * T, :]
            o = blk[:T, :] - lam * blk[T:, :]
            o = o * lax.rsqrt(jnp.mean(o * o, axis=1, keepdims=True) + LN_EPS) * g_ref[...] * (1.0 - lam_init)
            o_ref[:, h * vd:(h + 1) * vd] = o.astype(o_ref.dtype)


def _attn_decode(q, kb_new, vb_new, cache_k, cache_v, page_table, layer, slopes, lamv, norm_g,
                 batch, T, heads, vd, lam_init):
    depth, n_phys, page = cache_k.shape[:3]
    n_pages = page_table.shape[1]
    past = n_pages * page
    pages = math.gcd(PAGES_PER_STEP, n_pages)
    aw = heads * vd
    qd = vd // 2
    rows = heads * 2 * T
    assert 2 * T == SUBLANES
    q3 = q.reshape(batch, 1, T, aw)
    colmap = jnp.arange(aw, dtype=jnp.int32) // qd
    rowmap = jnp.arange(heads * 2, dtype=jnp.int32)
    q2 = jnp.where((colmap[None, :] == rowmap[:, None])[None, :, None, :], q3, jnp.zeros_like(q3))
    q2 = q2.reshape(batch, rows, aw)
    slope_rows = jnp.repeat(slopes, 2 * T).reshape(rows, 1)
    pad = lambda a: jnp.pad(a.reshape(batch, T, aw), ((0, 0), (0, SUBLANES - T), (0, 0)))
    ck = cache_k.reshape(depth * n_phys * page * heads, vd)
    cv = cache_v.reshape(depth * n_phys * page * heads, vd)
    page_spec = lambda i: pl.BlockSpec(
        (page * heads, vd), lambda b, c, pt: (layer * n_phys + pt[b * n_pages + c * pages + i], 0))
    const = lambda a: pl.BlockSpec(a.shape, lambda b, c, pt: (0,) * a.ndim)
    per_b = lambda r: pl.BlockSpec((None, r, aw), lambda b, c, pt: (b, 0, 0))
    grid_spec = pltpu.PrefetchScalarGridSpec(
        num_scalar_prefetch=1,
        grid=(batch, n_pages // pages),
        in_specs=[per_b(rows), per_b(SUBLANES), per_b(SUBLANES), const(slope_rows), const(lamv), const(norm_g)]
                 + [page_spec(i) for i in range(pages)] + [page_spec(i) for i in range(pages)],
        out_specs=per_b(T),
        scratch_shapes=[pltpu.VMEM((rows, 1), F32), pltpu.VMEM((rows, 1), F32), pltpu.VMEM((rows, aw), F32)],
    )
    return pl.pallas_call(
        functools.partial(_attn_decode_kernel, pages=pages, page=page, past=past, heads=heads, vd=vd, T=T,
                          lam_init=lam_init),
        grid_spec=grid_spec,
        out_shape=jax.ShapeDtypeStruct((batch, T, aw), F32),
        compiler_params=_params("parallel", "arbitrary"),
        name="attn_decode",
    )(page_table.reshape(-1), q2, pad(kb_new), pad(vb_new), slope_rows, lamv, norm_g,
      *([ck] * pages), *([cv] * pages))


def _layer_norm(x, g, b):
    mu = jnp.mean(x, axis=1, keepdims=True)
    xc = x - mu
    var = jnp.mean(xc * xc, axis=1, keepdims=True)
    return xc * lax.rsqrt(var + LN_EPS) * g + b


def _post_kernel(hm_ref, ha_ref, x_ref, wom_ref, woa_ref, l1g_ref, l1b_ref, wg_ref, wu_ref, wd_ref,
                 l2g_ref, l2b_ref, y_ref, *, alpha, ff_chunks):
    mix = (jnp.dot(hm_ref[...].astype(BF16), wom_ref[...], preferred_element_type=F32)
           + jnp.dot(ha_ref[...].astype(BF16), woa_ref[...], preferred_element_type=F32))
    x1 = _layer_norm(alpha * x_ref[...] + mix, l1g_ref[...], l1b_ref[...])
    x1b = x1.astype(BF16)
    fc = wg_ref.shape[1] // ff_chunks
    ffn = None
    for f in range(ff_chunks):
        gate = jnp.dot(x1b, wg_ref[:, f * fc:(f + 1) * fc], preferred_element_type=F32)
        up = jnp.dot(x1b, wu_ref[:, f * fc:(f + 1) * fc], preferred_element_type=F32)
        mid = (gate * _sigmoid(gate) * up).astype(BF16)
        part = jnp.dot(mid, wd_ref[f * fc:(f + 1) * fc, :], preferred_element_type=F32)
        ffn = part if ffn is None else ffn + part
    y_ref[...] = _layer_norm(alpha * x1 + ffn, l2g_ref[...], l2b_ref[...])


def _post(hm, ha, x2d, w_out_m, w_out_a, ln1_g, ln1_b, w_gate, w_up, w_down, ln2_g, ln2_b, alpha):
    rows, d = x2d.shape
    tm = min(ROW_TILE, rows)
    assert rows % tm == 0
    d_ff = w_gate.shape[1]
    ff_chunks = 2 if d_ff % (2 * LANES) == 0 else 1
    row = lambda a: pl.BlockSpec((tm, a.shape[1]), lambda i: (i, 0))
    const = lambda a: pl.BlockSpec(a.shape, lambda i: (0,) * a.ndim, pipeline_mode=pl.Buffered(1))
    return pl.pallas_call(
        functools.partial(_post_kernel, alpha=alpha, ff_chunks=ff_chunks),
        grid=(rows // tm,),
        in_specs=[row(hm), row(ha), row(x2d), const(w_out_m), const(w_out_a), const(ln1_g), const(ln1_b),
                  const(w_gate), const(w_up), const(w_down), const(ln2_g), const(ln2_b)],
        out_specs=pl.BlockSpec((tm, d), lambda i: (i, 0)),
        out_shape=jax.ShapeDtypeStruct((rows, d), F32),
        compiler_params=_params("parallel"),
        name="post_ffn",
    )(hm, ha, x2d, w_out_m, w_out_a, ln1_g, ln1_b, w_gate, w_up, w_down, ln2_g, ln2_b)


def kernel(x_prompt, x_sample, cache_k, cache_v, page_table, state_C, state_n, state_m, state_conv, w_in, b_ig, b_fg, conv_w, conv_b, m_norm_g, lam_q1, lam_k1, lam_q2, lam_k2, a_norm_g, w_out, ln1_g, ln1_b, w_gate, w_up, w_down, ln2_g, ln2_b):
    B, S, D = x_prompt.shape
    DB, T, _ = x_sample.shape
    depth = w_in.shape[0]
    heads = b_ig.shape[1]
    mw = m_norm_g.shape[1]
    hd = mw // heads
    vd = a_norm_g.shape[1]
    aw = w_out.shape[1] - mw
    a_heads = aw // vd
    alpha = (2.0 * depth) ** 0.25
    slopes = jnp.exp2(-8.0 * jnp.arange(1, a_heads + 1, dtype=F32) / a_heads)
    q_scale = (vd // 2) ** -0.5

    hp = x_prompt.reshape(B * S, D)
    hs = x_sample.reshape(DB * T, D)
    outs = [[] for _ in range(12)]
    for l in range(depth):
        lam_init = 0.8 - 0.6 * math.exp(-0.3 * l)
        w = w_in[l]
        g_lo = 4 * mw
        g_hi = g_lo + 2 * heads
        w_main = jnp.concatenate([w[:, :g_lo], w[:, g_hi:]], axis=1).astype(BF16)
        w_gates = jnp.pad(w[:, g_lo:g_hi], ((0, 0), (0, LANES - 2 * heads))).astype(BF16)
        gates_b = jnp.concatenate([b_ig[l], b_fg[l]]).astype(F32)
        gb_row = jnp.pad(gates_b, (0, LANES - 2 * heads)).reshape(1, LANES)
        gb_col = jnp.pad(gates_b, (0, SUBLANES - 2 * heads)).reshape(SUBLANES, 1)
        cw = conv_w[l].astype(F32)
        cb = conv_b[l].astype(F32).reshape(1, -1)
        ng = m_norm_g[l].astype(F32).reshape(1, mw)
        ag = a_norm_g[l].astype(F32).reshape(1, vd)
        lamv = jnp.stack([lam_q1[l], lam_k1[l], lam_q2[l], lam_k2[l]]).astype(F32)
        wo = w_out[l].astype(BF16)
        post_w = (wo[:mw], wo[mw:], ln1_g[l].reshape(1, D), ln1_b[l].reshape(1, D),
                  w_gate[l].astype(BF16), w_up[l].astype(BF16), w_down[l].astype(BF16),
                  ln2_g[l].reshape(1, D), ln2_b[l].reshape(1, D))

        qk, vm, om, g, gt, qa, ka, va, kb, vb = _project(hp, w_main, w_gates, mw, aw, a_heads, q_scale, S)
        hm, Cp, n_p, mp, tail = _mlstm_prompt(qk, vm, om, g, gt, cw, cb, gb_row, gb_col, ng, B, S, heads, hd)
        ha = _attn_prompt(qa, kb, vb, slopes, lamv, ag, B, S, a_heads, vd, lam_init)
        hp = _post(hm.reshape(B * S, mw), ha.reshape(B * S, aw), hp, *post_w, alpha)
        hist = cw.shape[0] - 1
        outs[0].append(ka.reshape(B, S, a_heads, vd))
        outs[1].append(va.reshape(B, S, a_heads, vd))
        outs[4].append(Cp)
        outs[5].append(n_p)
        outs[6].append(mp[:, :, 0])
        outs[7].append(tail[:, SUBLANES - hist:, :])

        qk, vm, om, g, gt, qa, ka, va, kb, vb = _project(hs, w_main, w_gates, mw, aw, a_heads, q_scale, DB * T)
        hm, Cs, n_s, ms = _mlstm_sample(qk, vm, om, g, gt, state_conv, state_C, state_n, state_m, l,
                                        cw, cb, gb_row, gb_col, ng, DB, T, heads, hd)
        ha = _attn_decode(qa, kb, vb, cache_k, cache_v, page_table, l, slopes, lamv, ag,
                          DB, T, a_heads, vd, lam_init)
        hs = _post(hm.reshape(DB * T, mw), ha.reshape(DB * T, aw), hs, *post_w, alpha)
        outs[2].append(ka.reshape(DB, T, a_heads, vd))
        outs[3].append(va.reshape(DB, T, a_heads, vd))
        outs[8].append(Cs)
        outs[9].append(n_s)
        outs[10].append(ms[:, :, 0])
        outs[11].append(qk.reshape(DB, T, 2 * mw)[:, T - hist:, :])

    stack = lambda o: o[0][None] if depth == 1 else jnp.stack(o)
    return (hp.reshape(B, S, D), hs.reshape(DB, T, D), *[stack(o) for o in outs])
```

```python
import functools
import math

import jax
import jax.numpy as jnp
from jax import lax
from jax.experimental import pallas as pl
from jax.experimental.pallas import tpu as pltpu

F32 = jnp.float32
BF16 = jnp.bfloat16

LN_EPS = 1e-5
M_CHUNK = 128
LANES = 128
SUBLANES = 8
ROW_TILE = 512
ATTN_ROWS = 256
MLSTM_GROUP = 2
PAGES_PER_STEP = 16
VMEM_LIMIT = 56 * 1024 * 1024

_NT = (((1,), (1,)), ((), ()))
_TN = (((0,), (0,)), ((), ()))


def _sigmoid(x):
    return 1.0 / (1.0 + jnp.exp(-x))


def _log_sigmoid(x):
    return jnp.minimum(x, 0.0) - jnp.log(1.0 + jnp.exp(-jnp.abs(x)))


def _params(*sem):
    return pltpu.CompilerParams(dimension_semantics=sem, vmem_limit_bytes=VMEM_LIMIT)


def _proj_kernel(x_ref, wm_ref, wg_ref, qk_ref, vm_ref, om_ref, g_ref, gt_ref,
                 q_ref, k_ref, v_ref, kb_ref, vb_ref, *, mw, aw, a_heads, q_scale):
    xb = x_ref[...].astype(BF16)
    tm = xb.shape[0]
    vd = aw // a_heads

    def seg(lo, n):
        return jnp.dot(xb, wm_ref[:, lo:lo + n], preferred_element_type=F32)

    def rows_by_head(ref, a):
        for h in range(a_heads):
            ref[pl.ds(h, tm, stride=a_heads), :] = a[:, h * vd:(h + 1) * vd]

    qk_ref[...] = seg(0, 2 * mw)
    vm_ref[...] = seg(2 * mw, mw).astype(BF16)
    om_ref[...] = seg(3 * mw, mw).astype(BF16)
    q_ref[...] = (seg(4 * mw, aw) * q_scale).astype(BF16)
    k = seg(4 * mw + aw, aw)
    rows_by_head(k_ref, k)
    kb_ref[...] = k.astype(BF16)
    v = seg(4 * mw + 2 * aw, aw)
    rows_by_head(v_ref, v)
    vb_ref[...] = v.astype(BF16)
    g = jnp.dot(xb, wg_ref[...], preferred_element_type=F32)
    g_ref[...] = g
    gt_ref[...] = g.T[:SUBLANES, :]


def _project(x2d, w_main, w_gate, mw, aw, a_heads, q_scale, seq):
    rows, d = x2d.shape
    tm = min(ROW_TILE, rows)
    assert rows % seq == 0 and seq % tm == 0
    tiles = seq // tm
    vd = aw // a_heads
    row = lambda n: pl.BlockSpec((tm, n), lambda i: (i, 0))
    by_head = pl.BlockSpec((tm * a_heads, vd), lambda i: (i, 0))
    const = lambda a: pl.BlockSpec(a.shape, lambda i: (0,) * a.ndim, pipeline_mode=pl.Buffered(1))
    out_shape = (
        jax.ShapeDtypeStruct((rows, 2 * mw), F32),
        jax.ShapeDtypeStruct((rows, mw), BF16),
        jax.ShapeDtypeStruct((rows, mw), BF16),
        jax.ShapeDtypeStruct((rows, LANES), F32),
        jax.ShapeDtypeStruct((rows // seq, SUBLANES, seq), F32),
        jax.ShapeDtypeStruct((rows, aw), BF16),
        jax.ShapeDtypeStruct((rows * a_heads, vd), F32),
        jax.ShapeDtypeStruct((rows * a_heads, vd), F32),
        jax.ShapeDtypeStruct((rows, aw), BF16),
        jax.ShapeDtypeStruct((rows, aw), BF16),
    )
    out_specs = (row(2 * mw), row(mw), row(mw), row(LANES),
                 pl.BlockSpec((None, SUBLANES, tm), lambda i: (i // tiles, 0, i % tiles)),
                 row(aw), by_head, by_head, row(aw), row(aw))
    return pl.pallas_call(
        functools.partial(_proj_kernel, mw=mw, aw=aw, a_heads=a_heads, q_scale=q_scale),
        grid=(rows // tm,),
        in_specs=[row(d), const(w_main), const(w_gate)],
        out_specs=out_specs,
        out_shape=out_shape,
        compiler_params=_params("parallel"),
        name="in_proj",
    )(x2d, w_main, w_gate)


def _conv_silu(ext_ref, length, cw_ref, cb_ref):
    width = cw_ref.shape[0]
    y = cb_ref[...] + cw_ref[width - 1:width, :] * ext_ref[pl.ds(SUBLANES, length), :]
    for j in range(width - 1):
        y = y + cw_ref[j:j + 1, :] * ext_ref[pl.ds(SUBLANES - (width - 1) + j, length), :]
    return y * _sigmoid(y)


def _mlstm_chunk(q, k, v, ig_col, ig_row, lf_col, lf_row, C, n, m):
    L = q.shape[0]
    row = lax.broadcasted_iota(jnp.int32, (L, L), 0)
    col = lax.broadcasted_iota(jnp.int32, (L, L), 1)
    tril = col <= row
    b_col = jnp.sum(jnp.where(tril, lf_row, 0.0), axis=1, keepdims=True)
    b_row = jnp.sum(jnp.where(row <= col, lf_col, 0.0), axis=0, keepdims=True)
    log_d = jnp.where(tril, b_col - b_row + ig_row, -jnp.inf)
    m_inter = b_col + m
    m_t = jnp.maximum(m_inter, jnp.max(log_d, axis=1, keepdims=True))
    d_w = jnp.exp(log_d - m_t)
    inter = jnp.exp(m_inter - m_t)
    d = q.shape[1]
    qb, kb, vb = q.astype(BF16), k.astype(BF16), v.astype(BF16)
    qk = lax.dot_general(qb, kb, _NT, preferred_element_type=F32)
    c_n = jnp.concatenate([C, jnp.broadcast_to(n, (d, d))], axis=0).astype(BF16)
    qc = lax.dot_general(qb, c_n, _NT, preferred_element_type=F32)
    yield None
    s = (qk * d_w).astype(BF16)
    v_1 = jnp.concatenate([vb, jnp.ones((L, d), BF16)], axis=1)
    nd = jnp.dot(s, v_1, preferred_element_type=F32) + inter * qc
    yield None
    h = nd[:, :d] / jnp.maximum(jnp.abs(nd[:, d:]), jnp.exp(-m_t))
    m_new = m_t[L - 1:L, :]
    b_last = b_col[L - 1:L, :]
    w_end = jnp.exp(b_last - b_col + ig_col - m_new)
    decay = jnp.exp(b_last + m - m_new)
    vw = (w_end * v.astype(F32)).astype(BF16)
    C_new = decay * C + lax.dot_general(vw, kb, _TN, preferred_element_type=F32)
    n_new = decay * n + jnp.sum(w_end * k, axis=0, keepdims=True)
    yield h, C_new, n_new, m_new


def _mlstm_heads(seqs, ng, *, heads, hd):
    mw = heads * hd
    k_scale = hd ** -0.5
    chains = []
    for y, vm, om, gc, gr, states in seqs:
        lf_c = _log_sigmoid(gc)
        lf_r = _log_sigmoid(gr)
        for h in range(heads):
            sl = slice(h * hd, (h + 1) * hd)
            chains.append(_mlstm_chunk(
                y[:, sl], y[:, mw + h * hd:mw + (h + 1) * hd] * k_scale, vm[:, sl],
                gc[:, h:h + 1], gr[h:h + 1, :], lf_c[:, heads + h:heads + h + 1],
                lf_r[heads + h:heads + h + 1, :], *states[h]))
    for chain in chains:
        next(chain)
    for chain in chains:
        next(chain)
    done = [next(chain) for chain in chains]
    out = []
    for s, (y, vm, om, gc, gr, states) in enumerate(seqs):
        res = []
        for h in range(heads):
            sl = slice(h * hd, (h + 1) * hd)
            hh, C_new, n_new, m_new = done[s * heads + h]
            hn = hh * lax.rsqrt(jnp.mean(hh * hh, axis=1, keepdims=True) + LN_EPS) * ng[:, sl]
            res.append((_sigmoid(om[:, sl].astype(F32)) * hn, C_new, n_new, m_new))
        out.append(res)
    return out


def _mlstm_store(results, hm_ref, C_ref, n_ref, m_ref, hd):
    for s, seq_results in enumerate(results):
        for h, (hm, C_new, n_new, m_new) in enumerate(seq_results):
            hm_ref[s, :, h * hd:(h + 1) * hd] = hm.astype(hm_ref.dtype)
            C_ref[s, h] = C_new
            n_ref[s, h:h + 1, :] = n_new
            m_ref[s, h:h + 1, :] = jnp.broadcast_to(m_new, (1, m_ref.shape[2]))


def _mlstm_prompt_kernel(qk_ref, vm_ref, om_ref, g_ref, gt_ref, cw_ref, cb_ref, gbr_ref, gbc_ref, ng_ref,
                         hm_ref, C_ref, n_ref, m_ref, tail_ref, ext_ref, *, heads, hd):
    c = pl.program_id(1)
    G, L = qk_ref.shape[0], qk_ref.shape[1]

    @pl.when(c == 0)
    def _():
        C_ref[...] = jnp.zeros_like(C_ref)
        n_ref[...] = jnp.zeros_like(n_ref)
        m_ref[...] = jnp.zeros_like(m_ref)
        ext_ref[:, 0:SUBLANES, :] = jnp.zeros((G, SUBLANES, ext_ref.shape[2]), F32)

    states = [[(C_ref[s, h], n_ref[s, h:h + 1, :], m_ref[s, h:h + 1, 0:1]) for h in range(heads)]
              for s in range(G)]
    seqs = []
    for s in range(G):
        u = qk_ref[s]
        ext = ext_ref.at[s]
        ext[pl.ds(SUBLANES, L), :] = u
        y = _conv_silu(ext, L, cw_ref, cb_ref)
        ext[0:SUBLANES, :] = u[L - SUBLANES:L, :]
        tail_ref[s] = u[L - SUBLANES:L, :]
        gc = g_ref[s] + gbr_ref[...]
        gr = gt_ref[s] + gbc_ref[...]
        seqs.append((y, vm_ref[s], om_ref[s], gc, gr, states[s]))
    results = _mlstm_heads(seqs, ng_ref[...], heads=heads, hd=hd)
    _mlstm_store(results, hm_ref, C_ref, n_ref, m_ref, hd)


def _mlstm_prompt(qk, vm, om, g, gt, conv_w, conv_b, gb_row, gb_col, norm_g, batch, seq, heads, hd):
    mw = heads * hd
    L = M_CHUNK if seq % M_CHUNK == 0 else seq
    nc = seq // L
    G = math.gcd(MLSTM_GROUP, batch)
    assert L % SUBLANES == 0 and 2 * heads <= SUBLANES and conv_w.shape[0] - 1 <= SUBLANES
    const = lambda a: pl.BlockSpec(a.shape, lambda b, c: (0,) * a.ndim)
    seq_spec = lambda n: pl.BlockSpec((G, L, n), lambda b, c: (b, c, 0))
    out_shape = (
        jax.ShapeDtypeStruct((batch, seq, mw), BF16),
        jax.ShapeDtypeStruct((batch, heads, hd, hd), F32),
        jax.ShapeDtypeStruct((batch, heads, hd), F32),
        jax.ShapeDtypeStruct((batch, heads, LANES), F32),
        jax.ShapeDtypeStruct((batch, SUBLANES, 2 * mw), F32),
    )
    out_specs = (
        seq_spec(mw),
        pl.BlockSpec((G, heads, hd, hd), lambda b, c: (b, 0, 0, 0)),
        pl.BlockSpec((G, heads, hd), lambda b, c: (b, 0, 0)),
        pl.BlockSpec((G, heads, LANES), lambda b, c: (b, 0, 0)),
        pl.BlockSpec((G, SUBLANES, 2 * mw), lambda b, c: (b, 0, 0)),
    )
    return pl.pallas_call(
        functools.partial(_mlstm_prompt_kernel, heads=heads, hd=hd),
        grid=(batch // G, nc),
        in_specs=[seq_spec(2 * mw), seq_spec(mw), seq_spec(mw), seq_spec(LANES),
                  pl.BlockSpec((G, SUBLANES, L), lambda b, c: (b, 0, c)),
                  const(conv_w), const(conv_b), const(gb_row), const(gb_col), const(norm_g)],
        out_specs=out_specs,
        out_shape=out_shape,
        scratch_shapes=[pltpu.VMEM((G, L + SUBLANES, 2 * mw), F32)],
        compiler_params=_params("parallel", "arbitrary"),
        name="mlstm_prompt",
    )(qk.reshape(batch, seq, 2 * mw), vm.reshape(batch, seq, mw), om.reshape(batch, seq, mw),
      g.reshape(batch, seq, LANES), gt, conv_w, conv_b, gb_row, gb_col, norm_g)


def _mlstm_sample_kernel(qk_ref, vm_ref, om_ref, g_ref, gt_ref, conv_ref, C0_ref, n0_ref, m0_ref,
                         cw_ref, cb_ref, gbr_ref, gbc_ref, ng_ref,
                         hm_ref, C_ref, n_ref, m_ref, ext_ref, *, heads, hd):
    G, T = qk_ref.shape[0], qk_ref.shape[1]
    hist = conv_ref.shape[1]
    seqs = []
    for s in range(G):
        ext = ext_ref.at[s]
        ext[pl.ds(SUBLANES - hist, hist), :] = conv_ref[s]
        ext[pl.ds(SUBLANES, T), :] = qk_ref[s]
        y = _conv_silu(ext, T, cw_ref, cb_ref)
        gc = g_ref[s] + gbr_ref[...]
        gr = gt_ref[s] + gbc_ref[...]
        states = [(C0_ref[s, h], n0_ref[s, h:h + 1, :], m0_ref[s, :, h:h + 1]) for h in range(heads)]
        seqs.append((y, vm_ref[s], om_ref[s], gc, gr, states))
    results = _mlstm_heads(seqs, ng_ref[...], heads=heads, hd=hd)
    _mlstm_store(results, hm_ref, C_ref, n_ref, m_ref, hd)


def _mlstm_sample(qk, vm, om, g, gt, state_conv, state_C, state_n, state_m, layer,
                  conv_w, conv_b, gb_row, gb_col, norm_g, batch, T, heads, hd):
    mw = heads * hd
    hist = state_conv.shape[-2]
    assert T <= M_CHUNK and hist == conv_w.shape[0] - 1 and hist <= SUBLANES and T <= SUBLANES
    gt3 = gt.reshape(SUBLANES, batch, T).transpose(1, 0, 2)
    G = math.gcd(MLSTM_GROUP, batch)
    const = lambda a: pl.BlockSpec(a.shape, lambda b: (0,) * a.ndim)
    tok = lambda n: pl.BlockSpec((G, T, n), lambda b: (b, 0, 0))
    lb = lambda b: layer * (batch // G) + b
    out_shape = (
        jax.ShapeDtypeStruct((batch, T, mw), F32),
        jax.ShapeDtypeStruct((batch, heads, hd, hd), F32),
        jax.ShapeDtypeStruct((batch, heads, hd), F32),
        jax.ShapeDtypeStruct((batch, heads, LANES), F32),
    )
    out_specs = (
        tok(mw),
        pl.BlockSpec((G, heads, hd, hd), lambda b: (b, 0, 0, 0)),
        pl.BlockSpec((G, heads, hd), lambda b: (b, 0, 0)),
        pl.BlockSpec((G, heads, LANES), lambda b: (b, 0, 0)),
    )
    depth = state_C.shape[0]
    return pl.pallas_call(
        functools.partial(_mlstm_sample_kernel, heads=heads, hd=hd),
        grid=(batch // G,),
        in_specs=[tok(2 * mw), tok(mw), tok(mw), tok(LANES),
                  pl.BlockSpec((G, SUBLANES, T), lambda b: (b, 0, 0)),
                  pl.BlockSpec((G, hist, 2 * mw), lambda b: (lb(b), 0, 0)),
                  pl.BlockSpec((G, heads, hd, hd), lambda b: (lb(b), 0, 0, 0)),
                  pl.BlockSpec((G, heads, hd), lambda b: (lb(b), 0, 0)),
                  pl.BlockSpec((G, 1, heads), lambda b: (lb(b), 0, 0)),
                  const(conv_w), const(conv_b), const(gb_row), const(gb_col), const(norm_g)],
        out_specs=out_specs,
        out_shape=out_shape,
        scratch_shapes=[pltpu.VMEM((G, 2 * SUBLANES, 2 * mw), F32)],
        compiler_params=_params("parallel"),
        name="mlstm_sample",
    )(qk.reshape(batch, T, 2 * mw), vm.reshape(batch, T, mw), om.reshape(batch, T, mw),
      g.reshape(batch, T, LANES), gt3,
      state_conv.reshape(depth * batch, hist, 2 * mw), state_C.reshape(depth * batch, heads, hd, hd),
      state_n.reshape(depth * batch, heads, hd), state_m.reshape(depth * batch, 1, heads),
      conv_w, conv_b, gb_row, gb_col, norm_g)


def _lambda(lamv_ref, lam_init):
    a = jnp.sum(lamv_ref[0:1, :] * lamv_ref[1:2, :], axis=1, keepdims=True)
    b = jnp.sum(lamv_ref[2:3, :] * lamv_ref[3:4, :], axis=1, keepdims=True)
    return jnp.exp(a) - jnp.exp(b) + lam_init


def _flash_update(s, v, m_sc, l_sc, acc_sc):
    m_prev = m_sc[...]
    m_new = jnp.maximum(m_prev, jnp.max(s, axis=1, keepdims=True))
    alpha = jnp.exp(m_prev - m_new)
    p = jnp.exp(s - m_new)
    l_sc[...] = alpha * l_sc[...] + jnp.sum(p, axis=1, keepdims=True)
    acc_sc[...] = alpha * acc_sc[...] + jnp.dot(p.astype(BF16), v, preferred_element_type=F32)
    m_sc[...] = m_new


def _attn_prompt_kernel(slopes_ref, q_ref, k_ref, v_ref, lamv_ref, g_ref, o_ref, va_sc, *, rc, qd, lam_init):
    seq, vd = v_ref.shape
    va_sc[:, :vd] = v_ref[...]
    va_sc[:, vd:] = jnp.ones((seq, va_sc.shape[1] - vd), va_sc.dtype)
    slope = slopes_ref[pl.program_id(1)]
    lam = _lambda(lamv_ref, lam_init)
    gain = g_ref[...] * (1.0 - lam_init)
    colf = lax.broadcasted_iota(jnp.int32, (1, seq), 1).astype(F32)
    lane = lax.broadcasted_iota(jnp.int32, (rc, vd), 1)
    tri = lax.broadcasted_iota(jnp.int32, (rc, rc), 1) <= lax.broadcasted_iota(jnp.int32, (rc, rc), 0)
    n_chunks = seq // rc
    order = list(range(0, n_chunks, 2)) + list(range(n_chunks - 1 - n_chunks % 2, 0, -2))
    for c in order:
        r0 = c * rc
        kw = r0 + rc
        q = q_ref[r0:kw, :]
        k = k_ref[0:kw, :]
        va = va_sc[0:kw, :]
        bias = slope * (colf[:, :kw] - float(r0))
        maps = []
        for first in (True, False):
            qm = jnp.where(lane < qd if first else lane >= qd, q, jnp.zeros_like(q))
            s = lax.dot_general(qm, k, _NT, preferred_element_type=F32) + bias
            last = jnp.where(tri, s[:, r0:], -jnp.inf)
            s = last if r0 == 0 else jnp.concatenate([s[:, :r0], last], axis=1)
            m = jnp.broadcast_to(jnp.max(s, axis=1, keepdims=True), (rc, LANES))
            p = jnp.exp(s - jnp.concatenate([m] * (kw // LANES), axis=1))
            pv = jnp.dot(p.astype(BF16), va, preferred_element_type=F32)
            maps.append(pv[:, :vd] / pv[:, vd:])
        o = maps[0] - lam * maps[1]
        o = o * lax.rsqrt(jnp.mean(o * o, axis=1, keepdims=True) + LN_EPS) * gain
        o_ref[r0:kw, :] = o.astype(o_ref.dtype)


def _attn_prompt(q, kb, vb, slopes, lamv, norm_g, batch, seq, heads, vd, lam_init):
    rc = min(ATTN_ROWS, seq)
    assert seq % rc == 0 and vd == LANES and 6 * seq * vd * 2 + 4 * rc * seq * 4 <= VMEM_LIMIT
    aw = heads * vd
    const = lambda a: pl.BlockSpec(a.shape, lambda b, h: (0,) * a.ndim)
    seq_head = pl.BlockSpec((None, seq, vd), lambda b, h: (b, 0, h))
    return pl.pallas_call(
        functools.partial(_attn_prompt_kernel, rc=rc, qd=vd // 2, lam_init=lam_init),
        grid=(batch, heads),
        in_specs=[pl.BlockSpec(memory_space=pltpu.SMEM), seq_head, seq_head, seq_head, const(lamv), const(norm_g)],
        out_specs=seq_head,
        out_shape=jax.ShapeDtypeStruct((batch, seq, aw), BF16),
        scratch_shapes=[pltpu.VMEM((seq, 2 * vd), BF16)],
        compiler_params=_params("parallel", "parallel"),
        name="attn_prompt",
    )(slopes, q.reshape(batch, seq, aw), kb.reshape(batch, seq, aw), vb.reshape(batch, seq, aw), lamv, norm_g)


def _attn_decode_kernel(pt_ref, q2_ref, kn_ref, vn_ref, slope_ref, lamv_ref, g_ref, *rest,
                        pages, page, past, heads, vd, T, lam_init):
    k_refs = rest[:pages]
    v_refs = rest[pages:2 * pages]
    o_ref, m_sc, l_sc, acc_sc = rest[2 * pages:]
    c = pl.program_id(1)
    nk = pages * page

    @pl.when(c == 0)
    def _():
        m_sc[...] = jnp.full_like(m_sc, -jnp.inf)
        l_sc[...] = jnp.zeros_like(l_sc)
        acc_sc[...] = jnp.zeros_like(acc_sc)

    q2 = q2_ref[...]
    slope = slope_ref[...]

    def page_rows(r):
        return jnp.concatenate([r[pl.ds(h, page, stride=heads), :] for h in range(heads)], axis=1).astype(BF16)

    kc = jnp.concatenate([page_rows(r) for r in k_refs], axis=0)
    vc = jnp.concatenate([page_rows(r) for r in v_refs], axis=0)
    s = lax.dot_general(q2, kc, _NT, preferred_element_type=F32)
    pos = (lax.broadcasted_iota(jnp.int32, (1, nk), 1) + (c * nk - past)).astype(F32)
    _flash_update(s + slope * pos, vc, m_sc, l_sc, acc_sc)

    @pl.when(c == pl.num_programs(1) - 1)
    def _():
        rows = q2.shape[0]
        kn = kn_ref[...]
        sn = lax.dot_general(q2, kn, _NT, preferred_element_type=F32)
        tk = lax.broadcasted_iota(jnp.int32, sn.shape, 1)
        tq = lax.broadcasted_iota(jnp.int32, sn.shape, 0) % T
        sn = jnp.where(tk <= tq, sn + slope * tk.astype(F32), -jnp.inf)
        _flash_update(sn, vn_ref[...], m_sc, l_sc, acc_sc)
        lam = _lambda(lamv_ref, lam_init)
        for h in range(heads):
            blk = acc_sc[h * 2 * T:(h + 1) * 2 * T, h * vd:(h + 1) * vd] / l_sc[h * 2 * T:(h + 1) * 2 * T, :]
            o = blk[:T, :] - lam * blk[T:, :]
            o = o * lax.rsqrt(jnp.mean(o * o, axis=1, keepdims=True) + LN_EPS) * g_ref[...] * (1.0 - lam_init)
            o_ref[:, h * vd:(h + 1) * vd] = o.astype(o_ref.dtype)


def _attn_decode(q, kb_new, vb_new, cache_k, cache_v, page_table, layer, slopes, lamv, norm_g,
                 batch, T, heads, vd, lam_init):
    depth, n_phys, page = cache_k.shape[:3]
    n_pages = page_table.shape[1]
    past = n_pages * page
    pages = math.gcd(PAGES_PER_STEP, n_pages)
    aw = heads * vd
    qd = vd // 2
    rows = heads * 2 * T
    assert 2 * T == SUBLANES
    q3 = q.reshape(batch, 1, T, aw)
    colmap = jnp.arange(aw, dtype=jnp.int32) // qd
    rowmap = jnp.arange(heads * 2, dtype=jnp.int32)
    q2 = jnp.where((colmap[None, :] == rowmap[:, None])[None, :, None, :], q3, jnp.zeros_like(q3))
    q2 = q2.reshape(batch, rows, aw)
    slope_rows = jnp.repeat(slopes, 2 * T).reshape(rows, 1)
    pad = lambda a: jnp.pad(a.reshape(batch, T, aw), ((0, 0), (0, SUBLANES - T), (0, 0)))
    ck = cache_k.reshape(depth * n_phys * page * heads, vd)
    cv = cache_v.reshape(depth * n_phys * page * heads, vd)
    page_spec = lambda i: pl.BlockSpec(
        (page * heads, vd), lambda b, c, pt: (layer * n_phys + pt[b * n_pages + c * pages + i], 0))
    const = lambda a: pl.BlockSpec(a.shape, lambda b, c, pt: (0,) * a.ndim)
    per_b = lambda r: pl.BlockSpec((None, r, aw), lambda b, c, pt: (b, 0, 0))
    grid_spec = pltpu.PrefetchScalarGridSpec(
        num_scalar_prefetch=1,
        grid=(batch, n_pages // pages),
        in_specs=[per_b(rows), per_b(SUBLANES), per_b(SUBLANES), const(slope_rows), const(lamv), const(norm_g)]
                 + [page_spec(i) for i in range(pages)] + [page_spec(i) for i in range(pages)],
        out_specs=per_b(T),
        scratch_shapes=[pltpu.VMEM((rows, 1), F32), pltpu.VMEM((rows, 1), F32), pltpu.VMEM((rows, aw), F32)],
    )
    return pl.pallas_call(
        functools.partial(_attn_decode_kernel, pages=pages, page=page, past=past, heads=heads, vd=vd, T=T,
                          lam_init=lam_init),
        grid_spec=grid_spec,
        out_shape=jax.ShapeDtypeStruct((batch, T, aw), F32),
        compiler_params=_params("parallel", "arbitrary"),
        name="attn_decode",
    )(page_table.reshape(-1), q2, pad(kb_new), pad(vb_new), slope_rows, lamv, norm_g,
      *([ck] * pages), *([cv] * pages))


def _layer_norm(x, g, b):
    mu = jnp.mean(x, axis=1, keepdims=True)
    xc = x - mu
    var = jnp.mean(xc * xc, axis=1, keepdims=True)
    return xc * lax.rsqrt(var + LN_EPS) * g + b


def _post_kernel(hm_ref, ha_ref, x_ref, wom_ref, woa_ref, l1g_ref, l1b_ref, wg_ref, wu_ref, wd_ref,
                 l2g_ref, l2b_ref, y_ref, *, alpha, ff_chunks):
    mix = (jnp.dot(hm_ref[...].astype(BF16), wom_ref[...], preferred_element_type=F32)
           + jnp.dot(ha_ref[...].astype(BF16), woa_ref[...], preferred_element_type=F32))
    x1 = _layer_norm(alpha * x_ref[...] + mix, l1g_ref[...], l1b_ref[...])
    x1b = x1.astype(BF16)
    fc = wg_ref.shape[1] // ff_chunks
    ffn = None
    for f in range(ff_chunks):
        gate = jnp.dot(x1b, wg_ref[:, f * fc:(f + 1) * fc], preferred_element_type=F32)
        up = jnp.dot(x1b, wu_ref[:, f * fc:(f + 1) * fc], preferred_element_type=F32)
        mid = (gate * _sigmoid(gate) * up).astype(BF16)
        part = jnp.dot(mid, wd_ref[f * fc:(f + 1) * fc, :], preferred_element_type=F32)
        ffn = part if ffn is None else ffn + part
    y_ref[...] = _layer_norm(alpha * x1 + ffn, l2g_ref[...], l2b_ref[...])


def _post(hm, ha, x2d, w_out_m, w_out_a, ln1_g, ln1_b, w_gate, w_up, w_down, ln2_g, ln2_b, alpha):
    rows, d = x2d.shape
    tm = min(ROW_TILE, rows)
    assert rows % tm == 0
    d_ff = w_gate.shape[1]
    ff_chunks = 2 if d_ff % (2 * LANES) == 0 else 1
    row = lambda a: pl.BlockSpec((tm, a.shape[1]), lambda i: (i, 0))
    const = lambda a: pl.BlockSpec(a.shape, lambda i: (0,) * a.ndim, pipeline_mode=pl.Buffered(1))
    return pl.pallas_call(
        functools.partial(_post_kernel, alpha=alpha, ff_chunks=ff_chunks),
        grid=(rows // tm,),
        in_specs=[row(hm), row(ha), row(x2d), const(w_out_m), const(w_out_a), const(ln1_g), const(ln1_b),
                  const(w_gate), const(w_up), const(w_down), const(ln2_g), const(ln2_b)],
        out_specs=pl.BlockSpec((tm, d), lambda i: (i, 0)),
        out_shape=jax.ShapeDtypeStruct((rows, d), F32),
        compiler_params=_params("parallel"),
        name="post_ffn",
    )(hm, ha, x2d, w_out_m, w_out_a, ln1_g, ln1_b, w_gate, w_up, w_down, ln2_g, ln2_b)


def kernel(x_prompt, x_sample, cache_k, cache_v, page_table, state_C, state_n, state_m, state_conv, w_in, b_ig, b_fg, conv_w, conv_b, m_norm_g, lam_q1, lam_k1, lam_q2, lam_k2, a_norm_g, w_out, ln1_g, ln1_b, w_gate, w_up, w_down, ln2_g, ln2_b):
    B, S, D = x_prompt.shape
    DB, T, _ = x_sample.shape
    depth = w_in.shape[0]
    heads = b_ig.shape[1]
    mw = m_norm_g.shape[1]
    hd = mw // heads
    vd = a_norm_g.shape[1]
    aw = w_out.shape[1] - mw
    a_heads = aw // vd
    alpha = (2.0 * depth) ** 0.25
    slopes = jnp.exp2(-8.0 * jnp.arange(1, a_heads + 1, dtype=F32) / a_heads)
    q_scale = (vd // 2) ** -0.5

    hp = x_prompt.reshape(B * S, D)
    hs = x_sample.reshape(DB * T, D)
    outs = [[] for _ in range(12)]
    for l in range(depth):
        lam_init = 0.8 - 0.6 * math.exp(-0.3 * l)
        w = w_in[l]
        g_lo = 4 * mw
        g_hi = g_lo + 2 * heads
        w_main = jnp.concatenate([w[:, :g_lo], w[:, g_hi:]], axis=1).astype(BF16)
        w_gates = jnp.pad(w[:, g_lo:g_hi], ((0, 0), (0, LANES - 2 * heads))).astype(BF16)
        gates_b = jnp.concatenate([b_ig[l], b_fg[l]]).astype(F32)
        gb_row = jnp.pad(gates_b, (0, LANES - 2 * heads)).reshape(1, LANES)
        gb_col = jnp.pad(gates_b, (0, SUBLANES - 2 * heads)).reshape(SUBLANES, 1)
        cw = conv_w[l].astype(F32)
        cb = conv_b[l].astype(F32).reshape(1, -1)
        ng = m_norm_g[l].astype(F32).reshape(1, mw)
        ag = a_norm_g[l].astype(F32).reshape(1, vd)
        lamv = jnp.stack([lam_q1[l], lam_k1[l], lam_q2[l], lam_k2[l]]).astype(F32)
        wo = w_out[l].astype(BF16)
        post_w = (wo[:mw], wo[mw:], ln1_g[l].reshape(1, D), ln1_b[l].reshape(1, D),
                  w_gate[l].astype(BF16), w_up[l].astype(BF16), w_down[l].astype(BF16),
                  ln2_g[l].reshape(1, D), ln2_b[l].reshape(1, D))

        qk, vm, om, g, gt, qa, ka, va, kb, vb = _project(hp, w_main, w_gates, mw, aw, a_heads, q_scale, S)
        hm, Cp, n_p, mp, tail = _mlstm_prompt(qk, vm, om, g, gt, cw, cb, gb_row, gb_col, ng, B, S, heads, hd)
        ha = _attn_prompt(qa, kb, vb, slopes, lamv, ag, B, S, a_heads, vd, lam_init)
        hp = _post(hm.reshape(B * S, mw), ha.reshape(B * S, aw), hp, *post_w, alpha)
        hist = cw.shape[0] - 1
        outs[0].append(ka.reshape(B, S, a_heads, vd))
        outs[1].append(va.reshape(B, S, a_heads, vd))
        outs[4].append(Cp)
        outs[5].append(n_p)
        outs[6].append(mp[:, :, 0])
        outs[7].append(tail[:, SUBLANES - hist:, :])

        qk, vm, om, g, gt, qa, ka, va, kb, vb = _project(hs, w_main, w_gates, mw, aw, a_heads, q_scale, DB * T)
        hm, Cs, n_s, ms = _mlstm_sample(qk, vm, om, g, gt, state_conv, state_C, state_n, state_m, l,
                                        cw, cb, gb_row, gb_col, ng, DB, T, heads, hd)
        ha = _attn_decode(qa, kb, vb, cache_k, cache_v, page_table, l, slopes, lamv, ag,
                          DB, T, a_heads, vd, lam_init)
        hs = _post(hm.reshape(DB * T, mw), ha.reshape(DB * T, aw), hs, *post_w, alpha)
        outs[2].append(ka.reshape(DB, T, a_heads, vd))
        outs[3].append(va.reshape(DB, T, a_heads, vd))
        outs[8].append(Cs)
        outs[9].append(n_s)
        outs[10].append(ms[:, :, 0])
        outs[11].append(qk.reshape(DB, T, 2 * mw)[:, T - hist:, :])

    stack = lambda o: o[0][None] if depth == 1 else jnp.stack(o)
    return (hp.reshape(B, S, D), hs.reshape(DB, T, D), *[stack(o) for o in outs])
```

```python
import functools
import math

import jax
import jax.numpy as jnp
from jax import lax
from jax.experimental import pallas as pl
from jax.experimental.pallas import tpu as pltpu

F32 = jnp.float32
BF16 = jnp.bfloat16

LN_EPS = 1e-5
M_CHUNK = 128
LANES = 128
SUBLANES = 8
ROW_TILE = 512
ATTN_ROWS = 256
MLSTM_GROUP = 2
PAGES_PER_STEP = 16
DECODE_GROUP = 4
VMEM_LIMIT = 56 * 1024 * 1024

_NT = (((1,), (1,)), ((), ()))
_TN = (((0,), (0,)), ((), ()))


def _sigmoid(x):
    return 1.0 / (1.0 + jnp.exp(-x))


def _log_sigmoid(x):
    return jnp.minimum(x, 0.0) - jnp.log(1.0 + jnp.exp(-jnp.abs(x)))


def _params(*sem):
    return pltpu.CompilerParams(dimension_semantics=sem, vmem_limit_bytes=VMEM_LIMIT)


def _proj_kernel(x_ref, wm_ref, wg_ref, qk_ref, vm_ref, om_ref, g_ref, gt_ref,
                 q_ref, k_ref, v_ref, kb_ref, vb_ref, *, mw, aw, a_heads, q_scale):
    xb = x_ref[...].astype(BF16)
    tm = xb.shape[0]
    vd = aw // a_heads

    def seg(lo, n):
        return jnp.dot(xb, wm_ref[:, lo:lo + n], preferred_element_type=F32)

    def rows_by_head(ref, a):
        for h in range(a_heads):
            ref[pl.ds(h, tm, stride=a_heads), :] = a[:, h * vd:(h + 1) * vd]

    qk_ref[...] = seg(0, 2 * mw)
    vm_ref[...] = seg(2 * mw, mw).astype(BF16)
    om_ref[...] = seg(3 * mw, mw).astype(BF16)
    q_ref[...] = (seg(4 * mw, aw) * q_scale).astype(BF16)
    k = seg(4 * mw + aw, aw)
    rows_by_head(k_ref, k)
    kb_ref[...] = k.astype(BF16)
    v = seg(4 * mw + 2 * aw, aw)
    rows_by_head(v_ref, v)
    vb_ref[...] = v.astype(BF16)
    g = jnp.dot(xb, wg_ref[...], preferred_element_type=F32)
    g_ref[...] = g
    gt_ref[...] = g.T[:SUBLANES, :]


def _project(x2d, w_main, w_gate, mw, aw, a_heads, q_scale, seq):
    rows, d = x2d.shape
    tm = min(ROW_TILE, rows)
    assert rows % seq == 0 and seq % tm == 0
    tiles = seq // tm
    vd = aw // a_heads
    row = lambda n: pl.BlockSpec((tm, n), lambda i: (i, 0))
    by_head = pl.BlockSpec((tm * a_heads, vd), lambda i: (i, 0))
    const = lambda a: pl.BlockSpec(a.shape, lambda i: (0,) * a.ndim, pipeline_mode=pl.Buffered(1))
    out_shape = (
        jax.ShapeDtypeStruct((rows, 2 * mw), F32),
        jax.ShapeDtypeStruct((rows, mw), BF16),
        jax.ShapeDtypeStruct((rows, mw), BF16),
        jax.ShapeDtypeStruct((rows, LANES), F32),
        jax.ShapeDtypeStruct((rows // seq, SUBLANES, seq), F32),
        jax.ShapeDtypeStruct((rows, aw), BF16),
        jax.ShapeDtypeStruct((rows * a_heads, vd), F32),
        jax.ShapeDtypeStruct((rows * a_heads, vd), F32),
        jax.ShapeDtypeStruct((rows, aw), BF16),
        jax.ShapeDtypeStruct((rows, aw), BF16),
    )
    out_specs = (row(2 * mw), row(mw), row(mw), row(LANES),
                 pl.BlockSpec((None, SUBLANES, tm), lambda i: (i // tiles, 0, i % tiles)),
                 row(aw), by_head, by_head, row(aw), row(aw))
    return pl.pallas_call(
        functools.partial(_proj_kernel, mw=mw, aw=aw, a_heads=a_heads, q_scale=q_scale),
        grid=(rows // tm,),
        in_specs=[row(d), const(w_main), const(w_gate)],
        out_specs=out_specs,
        out_shape=out_shape,
        compiler_params=_params("parallel"),
        name="in_proj",
    )(x2d, w_main, w_gate)


def _conv_silu(ext_ref, length, cw_ref, cb_ref):
    width = cw_ref.shape[0]
    y = cb_ref[...] + cw_ref[width - 1:width, :] * ext_ref[pl.ds(SUBLANES, length), :]
    for j in range(width - 1):
        y = y + cw_ref[j:j + 1, :] * ext_ref[pl.ds(SUBLANES - (width - 1) + j, length), :]
    return y * _sigmoid(y)


def _mlstm_chunk(q, k, v, ig_col, ig_row, lf_col, lf_row, C, n, m):
    L = q.shape[0]
    row = lax.broadcasted_iota(jnp.int32, (L, L), 0)
    col = lax.broadcasted_iota(jnp.int32, (L, L), 1)
    tril = col <= row
    b_col = jnp.sum(jnp.where(tril, lf_row, 0.0), axis=1, keepdims=True)
    b_row = jnp.sum(jnp.where(row <= col, lf_col, 0.0), axis=0, keepdims=True)
    log_d = jnp.where(tril, b_col - b_row + ig_row, -jnp.inf)
    m_inter = b_col + m
    m_t = jnp.maximum(m_inter, jnp.max(log_d, axis=1, keepdims=True))
    d_w = jnp.exp(log_d - m_t)
    inter = jnp.exp(m_inter - m_t)
    d = q.shape[1]
    qb, kb, vb = q.astype(BF16), k.astype(BF16), v.astype(BF16)
    qk = lax.dot_general(qb, kb, _NT, preferred_element_type=F32)
    c_n = jnp.concatenate([C, jnp.broadcast_to(n, (d, d))], axis=0).astype(BF16)
    qc = lax.dot_general(qb, c_n, _NT, preferred_element_type=F32)
    yield None
    s = (qk * d_w).astype(BF16)
    v_1 = jnp.concatenate([vb, jnp.ones((L, d), BF16)], axis=1)
    nd = jnp.dot(s, v_1, preferred_element_type=F32) + inter * qc
    yield None
    h = nd[:, :d] / jnp.maximum(jnp.abs(nd[:, d:]), jnp.exp(-m_t))
    m_new = m_t[L - 1:L, :]
    b_last = b_col[L - 1:L, :]
    w_end = jnp.exp(b_last - b_col + ig_col - m_new)
    decay = jnp.exp(b_last + m - m_new)
    vw = (w_end * v.astype(F32)).astype(BF16)
    C_new = decay * C + lax.dot_general(vw, kb, _TN, preferred_element_type=F32)
    n_new = decay * n + jnp.sum(w_end * k, axis=0, keepdims=True)
    yield h, C_new, n_new, m_new


def _mlstm_heads(seqs, ng, *, heads, hd):
    mw = heads * hd
    k_scale = hd ** -0.5
    chains = []
    for y, vm, om, gc, gr, states in seqs:
        lf_c = _log_sigmoid(gc)
        lf_r = _log_sigmoid(gr)
        for h in range(heads):
            sl = slice(h * hd, (h + 1) * hd)
            chains.append(_mlstm_chunk(
                y[:, sl], y[:, mw + h * hd:mw + (h + 1) * hd] * k_scale, vm[:, sl],
                gc[:, h:h + 1], gr[h:h + 1, :], lf_c[:, heads + h:heads + h + 1],
                lf_r[heads + h:heads + h + 1, :], *states[h]))
    for chain in chains:
        next(chain)
    for chain in chains:
        next(chain)
    done = [next(chain) for chain in chains]
    out = []
    for s, (y, vm, om, gc, gr, states) in enumerate(seqs):
        res = []
        for h in range(heads):
            sl = slice(h * hd, (h + 1) * hd)
            hh, C_new, n_new, m_new = done[s * heads + h]
            hn = hh * lax.rsqrt(jnp.mean(hh * hh, axis=1, keepdims=True) + LN_EPS) * ng[:, sl]
            res.append((_sigmoid(om[:, sl].astype(F32)) * hn, C_new, n_new, m_new))
        out.append(res)
    return out


def _mlstm_store(results, hm_ref, C_ref, n_ref, m_ref, hd):
    for s, seq_results in enumerate(results):
        for h, (hm, C_new, n_new, m_new) in enumerate(seq_results):
            hm_ref[s, :, h * hd:(h + 1) * hd] = hm.astype(hm_ref.dtype)
            C_ref[s, h] = C_new
            n_ref[s, h:h + 1, :] = n_new
            m_ref[s, h:h + 1, :] = jnp.broadcast_to(m_new, (1, m_ref.shape[2]))


def _mlstm_prompt_kernel(qk_ref, vm_ref, om_ref, g_ref, gt_ref, cw_ref, cb_ref, gbr_ref, gbc_ref, ng_ref,
                         hm_ref, C_ref, n_ref, m_ref, tail_ref, ext_ref, *, heads, hd):
    c = pl.program_id(1)
    G, L = qk_ref.shape[0], qk_ref.shape[1]

    @pl.when(c == 0)
    def _():
        C_ref[...] = jnp.zeros_like(C_ref)
        n_ref[...] = jnp.zeros_like(n_ref)
        m_ref[...] = jnp.zeros_like(m_ref)
        ext_ref[:, 0:SUBLANES, :] = jnp.zeros((G, SUBLANES, ext_ref.shape[2]), F32)

    states = [[(C_ref[s, h], n_ref[s, h:h + 1, :], m_ref[s, h:h + 1, 0:1]) for h in range(heads)]
              for s in range(G)]
    seqs = []
    for s in range(G):
        u = qk_ref[s]
        ext = ext_ref.at[s]
        ext[pl.ds(SUBLANES, L), :] = u
        y = _conv_silu(ext, L, cw_ref, cb_ref)
        ext[0:SUBLANES, :] = u[L - SUBLANES:L, :]
        tail_ref[s] = u[L - SUBLANES:L, :]
        gc = g_ref[s] + gbr_ref[...]
        gr = gt_ref[s] + gbc_ref[...]
        seqs.append((y, vm_ref[s], om_ref[s], gc, gr, states[s]))
    results = _mlstm_heads(seqs, ng_ref[...], heads=heads, hd=hd)
    _mlstm_store(results, hm_ref, C_ref, n_ref, m_ref, hd)


def _mlstm_prompt(qk, vm, om, g, gt, conv_w, conv_b, gb_row, gb_col, norm_g, batch, seq, heads, hd):
    mw = heads * hd
    L = M_CHUNK if seq % M_CHUNK == 0 else seq
    nc = seq // L
    G = math.gcd(MLSTM_GROUP, batch)
    assert L % SUBLANES == 0 and 2 * heads <= SUBLANES and conv_w.shape[0] - 1 <= SUBLANES
    const = lambda a: pl.BlockSpec(a.shape, lambda b, c: (0,) * a.ndim)
    seq_spec = lambda n: pl.BlockSpec((G, L, n), lambda b, c: (b, c, 0))
    out_shape = (
        jax.ShapeDtypeStruct((batch, seq, mw), BF16),
        jax.ShapeDtypeStruct((batch, heads, hd, hd), F32),
        jax.ShapeDtypeStruct((batch, heads, hd), F32),
        jax.ShapeDtypeStruct((batch, heads, LANES), F32),
        jax.ShapeDtypeStruct((batch, SUBLANES, 2 * mw), F32),
    )
    out_specs = (
        seq_spec(mw),
        pl.BlockSpec((G, heads, hd, hd), lambda b, c: (b, 0, 0, 0)),
        pl.BlockSpec((G, heads, hd), lambda b, c: (b, 0, 0)),
        pl.BlockSpec((G, heads, LANES), lambda b, c: (b, 0, 0)),
        pl.BlockSpec((G, SUBLANES, 2 * mw), lambda b, c: (b, 0, 0)),
    )
    return pl.pallas_call(
        functools.partial(_mlstm_prompt_kernel, heads=heads, hd=hd),
        grid=(batch // G, nc),
        in_specs=[seq_spec(2 * mw), seq_spec(mw), seq_spec(mw), seq_spec(LANES),
                  pl.BlockSpec((G, SUBLANES, L), lambda b, c: (b, 0, c)),
                  const(conv_w), const(conv_b), const(gb_row), const(gb_col), const(norm_g)],
        out_specs=out_specs,
        out_shape=out_shape,
        scratch_shapes=[pltpu.VMEM((G, L + SUBLANES, 2 * mw), F32)],
        compiler_params=_params("parallel", "arbitrary"),
        name="mlstm_prompt",
    )(qk.reshape(batch, seq, 2 * mw), vm.reshape(batch, seq, mw), om.reshape(batch, seq, mw),
      g.reshape(batch, seq, LANES), gt, conv_w, conv_b, gb_row, gb_col, norm_g)


def _mlstm_sample_kernel(qk_ref, vm_ref, om_ref, g_ref, gt_ref, conv_ref, C0_ref, n0_ref, m0_ref,
                         cw_ref, cb_ref, gbr_ref, gbc_ref, ng_ref,
                         hm_ref, C_ref, n_ref, m_ref, ext_ref, *, heads, hd):
    G, T = qk_ref.shape[0], qk_ref.shape[1]
    hist = conv_ref.shape[1]
    seqs = []
    for s in range(G):
        ext = ext_ref.at[s]
        ext[pl.ds(SUBLANES - hist, hist), :] = conv_ref[s]
        ext[pl.ds(SUBLANES, T), :] = qk_ref[s]
        y = _conv_silu(ext, T, cw_ref, cb_ref)
        gc = g_ref[s] + gbr_ref[...]
        gr = gt_ref[s] + gbc_ref[...]
        states = [(C0_ref[s, h], n0_ref[s, h:h + 1, :], m0_ref[s, :, h:h + 1]) for h in range(heads)]
        seqs.append((y, vm_ref[s], om_ref[s], gc, gr, states))
    results = _mlstm_heads(seqs, ng_ref[...], heads=heads, hd=hd)
    _mlstm_store(results, hm_ref, C_ref, n_ref, m_ref, hd)


def _mlstm_sample(qk, vm, om, g, gt, state_conv, state_C, state_n, state_m, layer,
                  conv_w, conv_b, gb_row, gb_col, norm_g, batch, T, heads, hd):
    mw = heads * hd
    hist = state_conv.shape[-2]
    assert T <= M_CHUNK and hist == conv_w.shape[0] - 1 and hist <= SUBLANES and T <= SUBLANES
    gt3 = gt.reshape(SUBLANES, batch, T).transpose(1, 0, 2)
    G = math.gcd(MLSTM_GROUP, batch)
    const = lambda a: pl.BlockSpec(a.shape, lambda b: (0,) * a.ndim)
    tok = lambda n: pl.BlockSpec((G, T, n), lambda b: (b, 0, 0))
    lb = lambda b: layer * (batch // G) + b
    out_shape = (
        jax.ShapeDtypeStruct((batch, T, mw), F32),
        jax.ShapeDtypeStruct((batch, heads, hd, hd), F32),
        jax.ShapeDtypeStruct((batch, heads, hd), F32),
        jax.ShapeDtypeStruct((batch, heads, LANES), F32),
    )
    out_specs = (
        tok(mw),
        pl.BlockSpec((G, heads, hd, hd), lambda b: (b, 0, 0, 0)),
        pl.BlockSpec((G, heads, hd), lambda b: (b, 0, 0)),
        pl.BlockSpec((G, heads, LANES), lambda b: (b, 0, 0)),
    )
    depth = state_C.shape[0]
    return pl.pallas_call(
        functools.partial(_mlstm_sample_kernel, heads=heads, hd=hd),
        grid=(batch // G,),
        in_specs=[tok(2 * mw), tok(mw), tok(mw), tok(LANES),
                  pl.BlockSpec((G, SUBLANES, T), lambda b: (b, 0, 0)),
                  pl.BlockSpec((G, hist, 2 * mw), lambda b: (lb(b), 0, 0)),
                  pl.BlockSpec((G, heads, hd, hd), lambda b: (lb(b), 0, 0, 0)),
                  pl.BlockSpec((G, heads, hd), lambda b: (lb(b), 0, 0)),
                  pl.BlockSpec((G, 1, heads), lambda b: (lb(b), 0, 0)),
                  const(conv_w), const(conv_b), const(gb_row), const(gb_col), const(norm_g)],
        out_specs=out_specs,
        out_shape=out_shape,
        scratch_shapes=[pltpu.VMEM((G, 2 * SUBLANES, 2 * mw), F32)],
        compiler_params=_params("parallel"),
        name="mlstm_sample",
    )(qk.reshape(batch, T, 2 * mw), vm.reshape(batch, T, mw), om.reshape(batch, T, mw),
      g.reshape(batch, T, LANES), gt3,
      state_conv.reshape(depth * batch, hist, 2 * mw), state_C.reshape(depth * batch, heads, hd, hd),
      state_n.reshape(depth * batch, heads, hd), state_m.reshape(depth * batch, 1, heads),
      conv_w, conv_b, gb_row, gb_col, norm_g)


def _lambda(lamv_ref, lam_init):
    a = jnp.sum(lamv_ref[0:1, :] * lamv_ref[1:2, :], axis=1, keepdims=True)
    b = jnp.sum(lamv_ref[2:3, :] * lamv_ref[3:4, :], axis=1, keepdims=True)
    return jnp.exp(a) - jnp.exp(b) + lam_init


def _flash_step(s, v, m_prev, l_prev, acc_prev):
    m_new = jnp.maximum(m_prev, jnp.max(s, axis=1, keepdims=True))
    alpha = jnp.exp(m_prev - m_new)
    p = jnp.exp(s - m_new)
    l_new = alpha * l_prev + jnp.sum(p, axis=1, keepdims=True)
    acc_new = alpha * acc_prev + jnp.dot(p.astype(BF16), v, preferred_element_type=F32)
    return m_new, l_new, acc_new


def _attn_prompt_kernel(slopes_ref, q_ref, k_ref, v_ref, lamv_ref, g_ref, o_ref, va_sc, *, rc, qd, lam_init):
    seq, vd = v_ref.shape
    va_sc[:, :vd] = v_ref[...]
    va_sc[:, vd:] = jnp.ones((seq, va_sc.shape[1] - vd), va_sc.dtype)
    slope = slopes_ref[pl.program_id(1)]
    lam = _lambda(lamv_ref, lam_init)
    gain = g_ref[...] * (1.0 - lam_init)
    colf = lax.broadcasted_iota(jnp.int32, (1, seq), 1).astype(F32)
    lane = lax.broadcasted_iota(jnp.int32, (rc, vd), 1)
    tri = (lax.broadcasted_iota(jnp.int32, (2 * rc, rc), 1)
           <= lax.broadcasted_iota(jnp.int32, (2 * rc, rc), 0) % rc)
    n_chunks = seq // rc
    order = list(range(0, n_chunks, 2)) + list(range(n_chunks - 1 - n_chunks % 2, 0, -2))
    for c in order:
        r0 = c * rc
        kw = r0 + rc
        q = q_ref[r0:kw, :]
        k = k_ref[0:kw, :]
        va = va_sc[0:kw, :]
        bias = slope * (colf[:, :kw] - float(r0))
        zero = jnp.zeros_like(q)
        qs = jnp.concatenate([jnp.where(lane < qd, q, zero), jnp.where(lane >= qd, q, zero)], axis=0)
        s = lax.dot_general(qs, k, _NT, preferred_element_type=F32) + bias
        last = jnp.where(tri, s[:, r0:], -jnp.inf)
        s = last if r0 == 0 else jnp.concatenate([s[:, :r0], last], axis=1)
        m = jnp.broadcast_to(jnp.max(s, axis=1, keepdims=True), (2 * rc, LANES))
        p = jnp.exp(s - jnp.concatenate([m] * (kw // LANES), axis=1))
        pv = jnp.dot(p.astype(BF16), va, preferred_element_type=F32)
        on = pv[:, :vd] / pv[:, vd:]
        o = on[:rc] - lam * on[rc:]
        o = o * lax.rsqrt(jnp.mean(o * o, axis=1, keepdims=True) + LN_EPS) * gain
        o_ref[r0:kw, :] = o.astype(o_ref.dtype)


def _attn_prompt(q, kb, vb, slopes, lamv, norm_g, batch, seq, heads, vd, lam_init):
    rc = min(ATTN_ROWS, seq)
    assert seq % rc == 0 and vd == LANES and 6 * seq * vd * 2 + 4 * rc * seq * 4 <= VMEM_LIMIT
    aw = heads * vd
    const = lambda a: pl.BlockSpec(a.shape, lambda b, h: (0,) * a.ndim)
    seq_head = pl.BlockSpec((None, seq, vd), lambda b, h: (b, 0, h))
    return pl.pallas_call(
        functools.partial(_attn_prompt_kernel, rc=rc, qd=vd // 2, lam_init=lam_init),
        grid=(batch, heads),
        in_specs=[pl.BlockSpec(memory_space=pltpu.SMEM), seq_head, seq_head, seq_head, const(lamv), const(norm_g)],
        out_specs=seq_head,
        out_shape=jax.ShapeDtypeStruct((batch, seq, aw), BF16),
        scratch_shapes=[pltpu.VMEM((seq, 2 * vd), BF16)],
        compiler_params=_params("parallel", "parallel"),
        name="attn_prompt",
    )(slopes, q.reshape(batch, seq, aw), kb.reshape(batch, seq, aw), vb.reshape(batch, seq, aw), lamv, norm_g)


def _attn_decode_kernel(pt_ref, q2_ref, kn_ref, vn_ref, slope_ref, lamv_ref, g_ref, *rest,
                        pages, page, past, heads, vd, T, lam_init):
    k_refs = rest[:pages]
    v_refs = rest[pages:2 * pages]
    o_ref, m_sc, l_sc, acc_sc = rest[2 * pages:]
    c = pl.program_id(1)
    nk = pages * page

    @pl.when(c == 0)
    def _():
        m_sc[...] = jnp.full_like(m_sc, -jnp.inf)
        l_sc[...] = jnp.zeros_like(l_sc)
        acc_sc[...] = jnp.zeros_like(acc_sc)

    q2 = q2_ref[...]
    slope = slope_ref[...]

    def page_rows(r):
        return jnp.concatenate([r[pl.ds(h, page, stride=heads), :] for h in range(heads)], axis=1).astype(BF16)

    group = math.gcd(DECODE_GROUP, pages)
    gk = group * page
    scores = []
    for g in range(pages // group):
        kc = jnp.concatenate([page_rows(r) for r in k_refs[g * group:(g + 1) * group]], axis=0)
        pos = (lax.broadcasted_iota(jnp.int32, (1, gk), 1) + (c * nk + g * gk - past)).astype(F32)
        scores.append(lax.dot_general(q2, kc, _NT, preferred_element_type=F32) + slope * pos)
    stats = (m_sc[...], l_sc[...], acc_sc[...])
    for g in range(pages // group):
        vc = jnp.concatenate([page_rows(r) for r in v_refs[g * group:(g + 1) * group]], axis=0)
        stats = _flash_step(scores[g], vc, *stats)
    m_sc[...], l_sc[...], acc_sc[...] = stats

    @pl.when(c == pl.num_programs(1) - 1)
    def _():
        kn = kn_ref[...]
        sn = lax.dot_general(q2, kn, _NT, preferred_element_type=F32)
        tk = lax.broadcasted_iota(jnp.int32, sn.shape, 1)
        tq = lax.broadcasted_iota(jnp.int32, sn.shape, 0) % T
        sn = jnp.where(tk <= tq, sn + slope * tk.astype(F32), -jnp.inf)
        _, l_fin, acc_fin = _flash_step(sn, vn_ref[...], m_sc[...], l_sc[...], acc_sc[...])
        lam = _lambda(lamv_ref, lam_init)
        for h in range(heads):
            blk = acc_fin[h * 2 * T:(h + 1) * 2 * T, h * vd:(h + 1) * vd] / l_fin[h * 2 * T:(h + 1) * 2 * T, :]
            o = blk[:T, :] - lam * blk[T:, :]
            o = o * lax.rsqrt(jnp.mean(o * o, axis=1, keepdims=True) + LN_EPS) * g_ref[...] * (1.0 - lam_init)
            o_ref[:, h * vd:(h + 1) * vd] = o.astype(o_ref.dtype)


def _attn_decode(q, kb_new, vb_new, cache_k, cache_v, page_table, layer, slopes, lamv, norm_g,
                 batch, T, heads, vd, lam_init):
    depth, n_phys, page = cache_k.shape[:3]
    n_pages = page_table.shape[1]
    past = n_pages * page
    pages = math.gcd(PAGES_PER_STEP, n_pages)
    aw = heads * vd
    qd = vd // 2
    rows = heads * 2 * T
    assert 2 * T == SUBLANES
    q3 = q.reshape(batch, 1, T, aw)
    colmap = jnp.arange(aw, dtype=jnp.int32) // qd
    rowmap = jnp.arange(heads * 2, dtype=jnp.int32)
    q2 = jnp.where((colmap[None, :] == rowmap[:, None])[None, :, None, :], q3, jnp.zeros_like(q3))
    q2 = q2.reshape(batch, rows, aw)
    slope_rows = jnp.repeat(slopes, 2 * T).reshape(rows, 1)
    pad = lambda a: jnp.pad(a.reshape(batch, T, aw), ((0, 0), (0, SUBLANES - T), (0, 0)))
    ck = cache_k.reshape(depth * n_phys * page * heads, vd)
    cv = cache_v.reshape(depth * n_phys * page * heads, vd)
    page_spec = lambda i: pl.BlockSpec(
        (page * heads, vd), lambda b, c, pt: (layer * n_phys + pt[b * n_pages + c * pages + i], 0))
    const = lambda a: pl.BlockSpec(a.shape, lambda b, c, pt: (0,) * a.ndim)
    per_b = lambda r: pl.BlockSpec((None, r, aw), lambda b, c, pt: (b, 0, 0))
    grid_spec = pltpu.PrefetchScalarGridSpec(
        num_scalar_prefetch=1,
        grid=(batch, n_pages // pages),
        in_specs=[per_b(rows), per_b(SUBLANES), per_b(SUBLANES), const(slope_rows), const(lamv), const(norm_g)]
                 + [page_spec(i) for i in range(pages)] + [page_spec(i) for i in range(pages)],
        out_specs=per_b(T),
        scratch_shapes=[pltpu.VMEM((rows, 1), F32), pltpu.VMEM((rows, 1), F32), pltpu.VMEM((rows, aw), F32)],
    )
    return pl.pallas_call(
        functools.partial(_attn_decode_kernel, pages=pages, page=page, past=past, heads=heads, vd=vd, T=T,
                          lam_init=lam_init),
        grid_spec=grid_spec,
        out_shape=jax.ShapeDtypeStruct((batch, T, aw), F32),
        compiler_params=_params("parallel", "arbitrary"),
        name="attn_decode",
    )(page_table.reshape(-1), q2, pad(kb_new), pad(vb_new), slope_rows, lamv, norm_g,
      *([ck] * pages), *([cv] * pages))


def _layer_norm(x, g, b):
    mu = jnp.mean(x, axis=1, keepdims=True)
    xc = x - mu
    var = jnp.mean(xc * xc, axis=1, keepdims=True)
    return xc * lax.rsqrt(var + LN_EPS) * g + b


def _post_kernel(hm_ref, ha_ref, x_ref, wom_ref, woa_ref, l1g_ref, l1b_ref, wg_ref, wu_ref, wd_ref,
                 l2g_ref, l2b_ref, y_ref, *, alpha, ff_chunks):
    mix = (jnp.dot(hm_ref[...].astype(BF16), wom_ref[...], preferred_element_type=F32)
           + jnp.dot(ha_ref[...].astype(BF16), woa_ref[...], preferred_element_type=F32))
    x1 = _layer_norm(alpha * x_ref[...] + mix, l1g_ref[...], l1b_ref[...])
    x1b = x1.astype(BF16)
    fc = wg_ref.shape[1] // ff_chunks
    ffn = None
    for f in range(ff_chunks):
        gate = jnp.dot(x1b, wg_ref[:, f * fc:(f + 1) * fc], preferred_element_type=F32)
        up = jnp.dot(x1b, wu_ref[:, f * fc:(f + 1) * fc], preferred_element_type=F32)
        mid = (gate * _sigmoid(gate) * up).astype(BF16)
        part = jnp.dot(mid, wd_ref[f * fc:(f + 1) * fc, :], preferred_element_type=F32)
        ffn = part if ffn is None else ffn + part
    y_ref[...] = _layer_norm(alpha * x1 + ffn, l2g_ref[...], l2b_ref[...])


def _post(hm, ha, x2d, w_out_m, w_out_a, ln1_g, ln1_b, w_gate, w_up, w_down, ln2_g, ln2_b, alpha):
    rows, d = x2d.shape
    tm = min(ROW_TILE, rows)
    assert rows % tm == 0
    d_ff = w_gate.shape[1]
    ff_chunks = 2 if d_ff % (2 * LANES) == 0 else 1
    row = lambda a: pl.BlockSpec((tm, a.shape[1]), lambda i: (i, 0))
    const = lambda a: pl.BlockSpec(a.shape, lambda i: (0,) * a.ndim, pipeline_mode=pl.Buffered(1))
    return pl.pallas_call(
        functools.partial(_post_kernel, alpha=alpha, ff_chunks=ff_chunks),
        grid=(rows // tm,),
        in_specs=[row(hm), row(ha), row(x2d), const(w_out_m), const(w_out_a), const(ln1_g), const(ln1_b),
                  const(w_gate), const(w_up), const(w_down), const(ln2_g), const(ln2_b)],
        out_specs=pl.BlockSpec((tm, d), lambda i: (i, 0)),
        out_shape=jax.ShapeDtypeStruct((rows, d), F32),
        compiler_params=_params("parallel"),
        name="post_ffn",
    )(hm, ha, x2d, w_out_m, w_out_a, ln1_g, ln1_b, w_gate, w_up, w_down, ln2_g, ln2_b)


def kernel(x_prompt, x_sample, cache_k, cache_v, page_table, state_C, state_n, state_m, state_conv, w_in, b_ig, b_fg, conv_w, conv_b, m_norm_g, lam_q1, lam_k1, lam_q2, lam_k2, a_norm_g, w_out, ln1_g, ln1_b, w_gate, w_up, w_down, ln2_g, ln2_b):
    B, S, D = x_prompt.shape
    DB, T, _ = x_sample.shape
    depth = w_in.shape[0]
    heads = b_ig.shape[1]
    mw = m_norm_g.shape[1]
    hd = mw // heads
    vd = a_norm_g.shape[1]
    aw = w_out.shape[1] - mw
    a_heads = aw // vd
    alpha = (2.0 * depth) ** 0.25
    slopes = jnp.exp2(-8.0 * jnp.arange(1, a_heads + 1, dtype=F32) / a_heads)
    q_scale = (vd // 2) ** -0.5

    hp = x_prompt.reshape(B * S, D)
    hs = x_sample.reshape(DB * T, D)
    outs = [[] for _ in range(12)]
    for l in range(depth):
        lam_init = 0.8 - 0.6 * math.exp(-0.3 * l)
        w = w_in[l]
        g_lo = 4 * mw
        g_hi = g_lo + 2 * heads
        w_main = jnp.concatenate([w[:, :g_lo], w[:, g_hi:]], axis=1).astype(BF16)
        w_gates = jnp.pad(w[:, g_lo:g_hi], ((0, 0), (0, LANES - 2 * heads))).astype(BF16)
        gates_b = jnp.concatenate([b_ig[l], b_fg[l]]).astype(F32)
        gb_row = jnp.pad(gates_b, (0, LANES - 2 * heads)).reshape(1, LANES)
        gb_col = jnp.pad(gates_b, (0, SUBLANES - 2 * heads)).reshape(SUBLANES, 1)
        cw = conv_w[l].astype(F32)
        cb = conv_b[l].astype(F32).reshape(1, -1)
        ng = m_norm_g[l].astype(F32).reshape(1, mw)
        ag = a_norm_g[l].astype(F32).reshape(1, vd)
        lamv = jnp.stack([lam_q1[l], lam_k1[l], lam_q2[l], lam_k2[l]]).astype(F32)
        wo = w_out[l].astype(BF16)
        post_w = (wo[:mw], wo[mw:], ln1_g[l].reshape(1, D), ln1_b[l].reshape(1, D),
                  w_gate[l].astype(BF16), w_up[l].astype(BF16), w_down[l].astype(BF16),
                  ln2_g[l].reshape(1, D), ln2_b[l].reshape(1, D))

        qk, vm, om, g, gt, qa, ka, va, kb, vb = _project(hp, w_main, w_gates, mw, aw, a_heads, q_scale, S)
        hm, Cp, n_p, mp, tail = _mlstm_prompt(qk, vm, om, g, gt, cw, cb, gb_row, gb_col, ng, B, S, heads, hd)
        ha = _attn_prompt(qa, kb, vb, slopes, lamv, ag, B, S, a_heads, vd, lam_init)
        hp = _post(hm.reshape(B * S, mw), ha.reshape(B * S, aw), hp, *post_w, alpha)
        hist = cw.shape[0] - 1
        outs[0].append(ka.reshape(B, S, a_heads, vd))
        outs[1].append(va.reshape(B, S, a_heads, vd))
        outs[4].append(Cp)
        outs[5].append(n_p)
        outs[6].append(mp[:, :, 0])
        outs[7].append(tail[:, SUBLANES - hist:, :])

        qk, vm, om, g, gt, qa, ka, va, kb, vb = _project(hs, w_main, w_gates, mw, aw, a_heads, q_scale, DB * T)
        hm, Cs, n_s, ms = _mlstm_sample(qk, vm, om, g, gt, state_conv, state_C, state_n, state_m, l,
                                        cw, cb, gb_row, gb_col, ng, DB, T, heads, hd)
        ha = _attn_decode(qa, kb, vb, cache_k, cache_v, page_table, l, slopes, lamv, ag,
                          DB, T, a_heads, vd, lam_init)
        hs = _post(hm.reshape(DB * T, mw), ha.reshape(DB * T, aw), hs, *post_w, alpha)
        outs[2].append(ka.reshape(DB, T, a_heads, vd))
        outs[3].append(va.reshape(DB, T, a_heads, vd))
        outs[8].append(Cs)
        outs[9].append(n_s)
        outs[10].append(ms[:, :, 0])
        outs[11].append(qk.reshape(DB, T, 2 * mw)[:, T - hist:, :])

    stack = lambda o: o[0][None] if depth == 1 else jnp.stack(o)
    return (hp.reshape(B, S, D), hs.reshape(DB, T, D), *[stack(o) for o in outs])
```

```python
import functools
import math

import jax
import jax.numpy as jnp
from jax import lax
from jax.experimental import pallas as pl
from jax.experimental.pallas import tpu as pltpu

F32 = jnp.float32
BF16 = jnp.bfloat16

LN_EPS = 1e-5
M_CHUNK = 128
LANES = 128
SUBLANES = 8
ROW_TILE = 512
ATTN_ROWS = 256
MLSTM_GROUP = 2
MXU_COLS = 256
PAGES_PER_STEP = 32
DECODE_GROUP = 4
VMEM_LIMIT = 56 * 1024 * 1024

_NT = (((1,), (1,)), ((), ()))
_TN = (((0,), (0,)), ((), ()))


def _sigmoid(x):
    return 1.0 / (1.0 + jnp.exp(-x))


def _log_sigmoid(x):
    return jnp.minimum(x, 0.0) - jnp.log(1.0 + jnp.exp(-jnp.abs(x)))


def _params(*sem):
    return pltpu.CompilerParams(dimension_semantics=sem, vmem_limit_bytes=VMEM_LIMIT)


def _proj_kernel(x_ref, wm_ref, wg_ref, qk_ref, vm_ref, om_ref, g_ref, gt_ref,
                 q_ref, k_ref, v_ref, kb_ref, vb_ref, *, mw, aw, a_heads, q_scale):
    xb = x_ref[...].astype(BF16)
    tm = xb.shape[0]
    vd = aw // a_heads

    def seg(lo, n):
        return jnp.dot(xb, wm_ref[:, lo:lo + n], preferred_element_type=F32)

    def rows_by_head(ref, a):
        for h in range(a_heads):
            ref[pl.ds(h, tm, stride=a_heads), :] = a[:, h * vd:(h + 1) * vd]

    qk_ref[...] = seg(0, 2 * mw)
    vm_ref[...] = seg(2 * mw, mw).astype(BF16)
    om_ref[...] = seg(3 * mw, mw).astype(BF16)
    q_ref[...] = (seg(4 * mw, aw) * q_scale).astype(BF16)
    k = seg(4 * mw + aw, aw)
    rows_by_head(k_ref, k)
    kb_ref[...] = k.astype(BF16)
    v = seg(4 * mw + 2 * aw, aw)
    rows_by_head(v_ref, v)
    vb_ref[...] = v.astype(BF16)
    g = jnp.dot(xb, wg_ref[...], preferred_element_type=F32)
    g_ref[...] = g
    gt_ref[...] = g.T[:SUBLANES, :]


def _project(x2d, w_main, w_gate, mw, aw, a_heads, q_scale, seq):
    rows, d = x2d.shape
    tm = min(ROW_TILE, rows)
    assert rows % seq == 0 and seq % tm == 0
    tiles = seq // tm
    vd = aw // a_heads
    row = lambda n: pl.BlockSpec((tm, n), lambda i: (i, 0))
    by_head = pl.BlockSpec((tm * a_heads, vd), lambda i: (i, 0))
    const = lambda a: pl.BlockSpec(a.shape, lambda i: (0,) * a.ndim, pipeline_mode=pl.Buffered(1))
    out_shape = (
        jax.ShapeDtypeStruct((rows, 2 * mw), F32),
        jax.ShapeDtypeStruct((rows, mw), BF16),
        jax.ShapeDtypeStruct((rows, mw), BF16),
        jax.ShapeDtypeStruct((rows, LANES), F32),
        jax.ShapeDtypeStruct((rows // seq, SUBLANES, seq), F32),
        jax.ShapeDtypeStruct((rows, aw), BF16),
        jax.ShapeDtypeStruct((rows * a_heads, vd), F32),
        jax.ShapeDtypeStruct((rows * a_heads, vd), F32),
        jax.ShapeDtypeStruct((rows, aw), BF16),
        jax.ShapeDtypeStruct((rows, aw), BF16),
    )
    out_specs = (row(2 * mw), row(mw), row(mw), row(LANES),
                 pl.BlockSpec((None, SUBLANES, tm), lambda i: (i // tiles, 0, i % tiles)),
                 row(aw), by_head, by_head, row(aw), row(aw))
    return pl.pallas_call(
        functools.partial(_proj_kernel, mw=mw, aw=aw, a_heads=a_heads, q_scale=q_scale),
        grid=(rows // tm,),
        in_specs=[row(d), const(w_main), const(w_gate)],
        out_specs=out_specs,
        out_shape=out_shape,
        compiler_params=_params("parallel"),
        name="in_proj",
    )(x2d, w_main, w_gate)


def _conv_silu(ext_ref, length, cw_ref, cb_ref):
    width = cw_ref.shape[0]
    y = cb_ref[...] + cw_ref[width - 1:width, :] * ext_ref[pl.ds(SUBLANES, length), :]
    for j in range(width - 1):
        y = y + cw_ref[j:j + 1, :] * ext_ref[pl.ds(SUBLANES - (width - 1) + j, length), :]
    return y * _sigmoid(y)


def _mlstm_chunk(q, k, v, ig_col, ig_row, lf_col, lf_row, C, n, m):
    L = q.shape[0]
    row = lax.broadcasted_iota(jnp.int32, (L, L), 0)
    col = lax.broadcasted_iota(jnp.int32, (L, L), 1)
    tril = col <= row
    b_col = jnp.sum(jnp.where(tril, lf_row, 0.0), axis=1, keepdims=True)
    b_row = jnp.sum(jnp.where(row <= col, lf_col, 0.0), axis=0, keepdims=True)
    log_d = jnp.where(tril, b_col - b_row + ig_row, -jnp.inf)
    m_inter = b_col + m
    m_t = jnp.maximum(m_inter, jnp.max(log_d, axis=1, keepdims=True))
    d_w = jnp.exp(log_d - m_t)
    inter = jnp.exp(m_inter - m_t)
    d = q.shape[1]
    qb, kb, vb = q.astype(BF16), k.astype(BF16), v.astype(BF16)
    qk = lax.dot_general(qb, kb, _NT, preferred_element_type=F32)
    c_n = jnp.concatenate([C, jnp.broadcast_to(n, (d, d))], axis=0).astype(BF16)
    qc = lax.dot_general(qb, c_n, _NT, preferred_element_type=F32)
    yield None
    s = (qk * d_w).astype(BF16)
    v_1 = jnp.concatenate([vb, jnp.ones((L, d), BF16)], axis=1)
    nd = jnp.dot(s, v_1, preferred_element_type=F32) + inter * qc
    yield None
    h = nd[:, :d] / jnp.maximum(jnp.abs(nd[:, d:]), jnp.exp(-m_t))
    m_new = m_t[L - 1:L, :]
    b_last = b_col[L - 1:L, :]
    w_end = jnp.exp(b_last - b_col + ig_col - m_new)
    decay = jnp.exp(b_last + m - m_new)
    vw = (w_end * v.astype(F32)).astype(BF16)
    C_new = decay * C + lax.dot_general(vw, kb, _TN, preferred_element_type=F32)
    n_new = decay * n + jnp.sum(w_end * k, axis=0, keepdims=True)
    yield h, C_new, n_new, m_new


def _mlstm_heads(seqs, ng, *, heads, hd):
    mw = heads * hd
    k_scale = hd ** -0.5
    chains = []
    for y, vm, om, gc, gr, states in seqs:
        lf_c = _log_sigmoid(gc)
        lf_r = _log_sigmoid(gr)
        for h in range(heads):
            sl = slice(h * hd, (h + 1) * hd)
            chains.append(_mlstm_chunk(
                y[:, sl], y[:, mw + h * hd:mw + (h + 1) * hd] * k_scale, vm[:, sl],
                gc[:, h:h + 1], gr[h:h + 1, :], lf_c[:, heads + h:heads + h + 1],
                lf_r[heads + h:heads + h + 1, :], *states[h]))
    for chain in chains:
        next(chain)
    for chain in chains:
        next(chain)
    done = [next(chain) for chain in chains]
    out = []
    for s, (y, vm, om, gc, gr, states) in enumerate(seqs):
        res = []
        for h in range(heads):
            sl = slice(h * hd, (h + 1) * hd)
            hh, C_new, n_new, m_new = done[s * heads + h]
            hn = hh * lax.rsqrt(jnp.mean(hh * hh, axis=1, keepdims=True) + LN_EPS) * ng[:, sl]
            res.append((_sigmoid(om[:, sl].astype(F32)) * hn, C_new, n_new, m_new))
        out.append(res)
    return out


def _mlstm_store(results, hm_ref, C_ref, n_ref, m_ref, hd):
    for s, seq_results in enumerate(results):
        for h, (hm, C_new, n_new, m_new) in enumerate(seq_results):
            hm_ref[s, :, h * hd:(h + 1) * hd] = hm.astype(hm_ref.dtype)
            C_ref[s, h] = C_new
            n_ref[s, h:h + 1, :] = n_new
            m_ref[s, h:h + 1, :] = jnp.broadcast_to(m_new, (1, m_ref.shape[2]))


def _mlstm_prompt_kernel(qk_ref, vm_ref, om_ref, g_ref, gt_ref, cw_ref, cb_ref, gbr_ref, gbc_ref, ng_ref,
                         hm_ref, C_ref, n_ref, m_ref, tail_ref, ext_ref, *, heads, hd):
    c = pl.program_id(1)
    G, L = qk_ref.shape[0], qk_ref.shape[1]

    @pl.when(c == 0)
    def _():
        C_ref[...] = jnp.zeros_like(C_ref)
        n_ref[...] = jnp.zeros_like(n_ref)
        m_ref[...] = jnp.zeros_like(m_ref)
        ext_ref[:, 0:SUBLANES, :] = jnp.zeros((G, SUBLANES, ext_ref.shape[2]), F32)

    states = [[(C_ref[s, h], n_ref[s, h:h + 1, :], m_ref[s, h:h + 1, 0:1]) for h in range(heads)]
              for s in range(G)]
    seqs = []
    for s in range(G):
        u = qk_ref[s]
        ext = ext_ref.at[s]
        ext[pl.ds(SUBLANES, L), :] = u
        y = _conv_silu(ext, L, cw_ref, cb_ref)
        ext[0:SUBLANES, :] = u[L - SUBLANES:L, :]
        tail_ref[s] = u[L - SUBLANES:L, :]
        gc = g_ref[s] + gbr_ref[...]
        gr = gt_ref[s] + gbc_ref[...]
        seqs.append((y, vm_ref[s], om_ref[s], gc, gr, states[s]))
    results = _mlstm_heads(seqs, ng_ref[...], heads=heads, hd=hd)
    _mlstm_store(results, hm_ref, C_ref, n_ref, m_ref, hd)


def _mlstm_prompt(qk, vm, om, g, gt, conv_w, conv_b, gb_row, gb_col, norm_g, batch, seq, heads, hd):
    mw = heads * hd
    L = M_CHUNK if seq % M_CHUNK == 0 else seq
    nc = seq // L
    G = math.gcd(MLSTM_GROUP, batch)
    assert L % SUBLANES == 0 and 2 * heads <= SUBLANES and conv_w.shape[0] - 1 <= SUBLANES
    const = lambda a: pl.BlockSpec(a.shape, lambda b, c: (0,) * a.ndim)
    seq_spec = lambda n: pl.BlockSpec((G, L, n), lambda b, c: (b, c, 0))
    out_shape = (
        jax.ShapeDtypeStruct((batch, seq, mw), BF16),
        jax.ShapeDtypeStruct((batch, heads, hd, hd), F32),
        jax.ShapeDtypeStruct((batch, heads, hd), F32),
        jax.ShapeDtypeStruct((batch, heads, LANES), F32),
        jax.ShapeDtypeStruct((batch, SUBLANES, 2 * mw), F32),
    )
    out_specs = (
        seq_spec(mw),
        pl.BlockSpec((G, heads, hd, hd), lambda b, c: (b, 0, 0, 0)),
        pl.BlockSpec((G, heads, hd), lambda b, c: (b, 0, 0)),
        pl.BlockSpec((G, heads, LANES), lambda b, c: (b, 0, 0)),
        pl.BlockSpec((G, SUBLANES, 2 * mw), lambda b, c: (b, 0, 0)),
    )
    return pl.pallas_call(
        functools.partial(_mlstm_prompt_kernel, heads=heads, hd=hd),
        grid=(batch // G, nc),
        in_specs=[seq_spec(2 * mw), seq_spec(mw), seq_spec(mw), seq_spec(LANES),
                  pl.BlockSpec((G, SUBLANES, L), lambda b, c: (b, 0, c)),
                  const(conv_w), const(conv_b), const(gb_row), const(gb_col), const(norm_g)],
        out_specs=out_specs,
        out_shape=out_shape,
        scratch_shapes=[pltpu.VMEM((G, L + SUBLANES, 2 * mw), F32)],
        compiler_params=_params("parallel", "arbitrary"),
        name="mlstm_prompt",
    )(qk.reshape(batch, seq, 2 * mw), vm.reshape(batch, seq, mw), om.reshape(batch, seq, mw),
      g.reshape(batch, seq, LANES), gt, conv_w, conv_b, gb_row, gb_col, norm_g)


def _mlstm_sample_kernel(qk_ref, vm_ref, om_ref, g_ref, gt_ref, conv_ref, C0_ref, n0_ref, m0_ref,
                         cw_ref, cb_ref, gbr_ref, gbc_ref, ng_ref,
                         hm_ref, C_ref, n_ref, m_ref, ext_ref, *, heads, hd):
    G, T = qk_ref.shape[0], qk_ref.shape[1]
    hist = conv_ref.shape[1]
    seqs = []
    for s in range(G):
        ext = ext_ref.at[s]
        ext[pl.ds(SUBLANES - hist, hist), :] = conv_ref[s]
        ext[pl.ds(SUBLANES, T), :] = qk_ref[s]
        y = _conv_silu(ext, T, cw_ref, cb_ref)
        gc = g_ref[s] + gbr_ref[...]
        gr = gt_ref[s] + gbc_ref[...]
        states = [(C0_ref[s, h], n0_ref[s, h:h + 1, :], m0_ref[s, :, h:h + 1]) for h in range(heads)]
        seqs.append((y, vm_ref[s], om_ref[s], gc, gr, states))
    results = _mlstm_heads(seqs, ng_ref[...], heads=heads, hd=hd)
    _mlstm_store(results, hm_ref, C_ref, n_ref, m_ref, hd)


def _mlstm_sample(qk, vm, om, g, gt, state_conv, state_C, state_n, state_m, layer,
                  conv_w, conv_b, gb_row, gb_col, norm_g, batch, T, heads, hd):
    mw = heads * hd
    hist = state_conv.shape[-2]
    assert T <= M_CHUNK and hist == conv_w.shape[0] - 1 and hist <= SUBLANES and T <= SUBLANES
    gt3 = gt.reshape(SUBLANES, batch, T).transpose(1, 0, 2)
    G = math.gcd(MLSTM_GROUP, batch)
    const = lambda a: pl.BlockSpec(a.shape, lambda b: (0,) * a.ndim)
    tok = lambda n: pl.BlockSpec((G, T, n), lambda b: (b, 0, 0))
    lb = lambda b: layer * (batch // G) + b
    out_shape = (
        jax.ShapeDtypeStruct((batch, T, mw), F32),
        jax.ShapeDtypeStruct((batch, heads, hd, hd), F32),
        jax.ShapeDtypeStruct((batch, heads, hd), F32),
        jax.ShapeDtypeStruct((batch, heads, LANES), F32),
    )
    out_specs = (
        tok(mw),
        pl.BlockSpec((G, heads, hd, hd), lambda b: (b, 0, 0, 0)),
        pl.BlockSpec((G, heads, hd), lambda b: (b, 0, 0)),
        pl.BlockSpec((G, heads, LANES), lambda b: (b, 0, 0)),
    )
    depth = state_C.shape[0]
    return pl.pallas_call(
        functools.partial(_mlstm_sample_kernel, heads=heads, hd=hd),
        grid=(batch // G,),
        in_specs=[tok(2 * mw), tok(mw), tok(mw), tok(LANES),
                  pl.BlockSpec((G, SUBLANES, T), lambda b: (b, 0, 0)),
                  pl.BlockSpec((G, hist, 2 * mw), lambda b: (lb(b), 0, 0)),
                  pl.BlockSpec((G, heads, hd, hd), lambda b: (lb(b), 0, 0, 0)),
                  pl.BlockSpec((G, heads, hd), lambda b: (lb(b), 0, 0)),
                  pl.BlockSpec((G, 1, heads), lambda b: (lb(b), 0, 0)),
                  const(conv_w), const(conv_b), const(gb_row), const(gb_col), const(norm_g)],
        out_specs=out_specs,
        out_shape=out_shape,
        scratch_shapes=[pltpu.VMEM((G, 2 * SUBLANES, 2 * mw), F32)],
        compiler_params=_params("parallel"),
        name="mlstm_sample",
    )(qk.reshape(batch, T, 2 * mw), vm.reshape(batch, T, mw), om.reshape(batch, T, mw),
      g.reshape(batch, T, LANES), gt3,
      state_conv.reshape(depth * batch, hist, 2 * mw), state_C.reshape(depth * batch, heads, hd, hd),
      state_n.reshape(depth * batch, heads, hd), state_m.reshape(depth * batch, 1, heads),
      conv_w, conv_b, gb_row, gb_col, norm_g)


def _lambda(lamv_ref, lam_init):
    a = jnp.sum(lamv_ref[0:1, :] * lamv_ref[1:2, :], axis=1, keepdims=True)
    b = jnp.sum(lamv_ref[2:3, :] * lamv_ref[3:4, :], axis=1, keepdims=True)
    return jnp.exp(a) - jnp.exp(b) + lam_init


def _flash_step(s, v, m_prev, l_prev, acc_prev):
    m_new = jnp.maximum(m_prev, jnp.max(s, axis=1, keepdims=True))
    alpha = jnp.exp(m_prev - m_new)
    p = jnp.exp(s - m_new)
    l_new = alpha * l_prev + jnp.sum(p, axis=1, keepdims=True)
    acc_new = alpha * acc_prev + jnp.dot(p.astype(BF16), v, preferred_element_type=F32)
    return m_new, l_new, acc_new


def _attn_prompt_kernel(slopes_ref, q_ref, k_ref, v_ref, lamv_ref, g_ref, o_ref, va_sc, *, rc, qd, lam_init):
    seq, vd = v_ref.shape
    va_sc[:, :vd] = v_ref[...]
    va_sc[:, vd:] = jnp.ones((seq, va_sc.shape[1] - vd), va_sc.dtype)
    slope = slopes_ref[pl.program_id(1)]
    lam = _lambda(lamv_ref, lam_init)
    gain = g_ref[...] * (1.0 - lam_init)
    colf = lax.broadcasted_iota(jnp.int32, (1, seq), 1).astype(F32)
    lane = lax.broadcasted_iota(jnp.int32, (rc, vd), 1)
    tri = (lax.broadcasted_iota(jnp.int32, (2 * rc, rc), 1)
           <= lax.broadcasted_iota(jnp.int32, (2 * rc, rc), 0) % rc)
    n_chunks = seq // rc
    order = list(range(0, n_chunks, 2)) + list(range(n_chunks - 1 - n_chunks % 2, 0, -2))
    for c in order:
        r0 = c * rc
        kw = r0 + rc
        q = q_ref[r0:kw, :]
        k = k_ref[0:kw, :]
        va = va_sc[0:kw, :]
        bias = slope * (colf[:, :kw] - float(r0))
        zero = jnp.zeros_like(q)
        qs = jnp.concatenate([jnp.where(lane < qd, q, zero), jnp.where(lane >= qd, q, zero)], axis=0)
        s = lax.dot_general(qs, k, _NT, preferred_element_type=F32) + bias
        last = jnp.where(tri, s[:, r0:], -jnp.inf)
        s = last if r0 == 0 else jnp.concatenate([s[:, :r0], last], axis=1)
        m = jnp.broadcast_to(jnp.max(s, axis=1, keepdims=True), (2 * rc, LANES))
        p = jnp.exp(s - jnp.concatenate([m] * (kw // LANES), axis=1))
        pv = jnp.dot(p.astype(BF16), va, preferred_element_type=F32)
        on = pv[:, :vd] / pv[:, vd:]
        o = on[:rc] - lam * on[rc:]
        o = o * lax.rsqrt(jnp.mean(o * o, axis=1, keepdims=True) + LN_EPS) * gain
        o_ref[r0:kw, :] = o.astype(o_ref.dtype)


def _attn_prompt(q, kb, vb, slopes, lamv, norm_g, batch, seq, heads, vd, lam_init):
    rc = min(ATTN_ROWS, seq)
    assert seq % rc == 0 and vd == LANES and 6 * seq * vd * 2 + 4 * rc * seq * 4 <= VMEM_LIMIT
    aw = heads * vd
    const = lambda a: pl.BlockSpec(a.shape, lambda b, h: (0,) * a.ndim)
    seq_head = pl.BlockSpec((None, seq, vd), lambda b, h: (b, 0, h))
    return pl.pallas_call(
        functools.partial(_attn_prompt_kernel, rc=rc, qd=vd // 2, lam_init=lam_init),
        grid=(batch, heads),
        in_specs=[pl.BlockSpec(memory_space=pltpu.SMEM), seq_head, seq_head, seq_head, const(lamv), const(norm_g)],
        out_specs=seq_head,
        out_shape=jax.ShapeDtypeStruct((batch, seq, aw), BF16),
        scratch_shapes=[pltpu.VMEM((seq, 2 * vd), BF16)],
        compiler_params=_params("parallel", "parallel"),
        name="attn_prompt",
    )(slopes, q.reshape(batch, seq, aw), kb.reshape(batch, seq, aw), vb.reshape(batch, seq, aw), lamv, norm_g)


def _attn_decode_kernel(pt_ref, q2_ref, kn_ref, vn_ref, slope_ref, lamv_ref, g_ref, *rest,
                        pages, page, past, heads, vd, T, lam_init):
    k_refs = rest[:pages]
    v_refs = rest[pages:2 * pages]
    o_ref, m_sc, l_sc, acc_sc = rest[2 * pages:]
    c = pl.program_id(1)
    nk = pages * page

    @pl.when(c == 0)
    def _():
        m_sc[...] = jnp.full_like(m_sc, -jnp.inf)
        l_sc[...] = jnp.zeros_like(l_sc)
        acc_sc[...] = jnp.zeros_like(acc_sc)

    q2 = q2_ref[...]
    slope = slope_ref[...]

    def page_rows(r):
        return jnp.concatenate([r[pl.ds(h, page, stride=heads), :] for h in range(heads)], axis=1).astype(BF16)

    group = math.gcd(DECODE_GROUP, pages)
    gk = group * page
    scores = []
    for g in range(pages // group):
        kc = jnp.concatenate([page_rows(r) for r in k_refs[g * group:(g + 1) * group]], axis=0)
        pos = (lax.broadcasted_iota(jnp.int32, (1, gk), 1) + (c * nk + g * gk - past)).astype(F32)
        scores.append(lax.dot_general(q2, kc, _NT, preferred_element_type=F32) + slope * pos)
    stats = (m_sc[...], l_sc[...], acc_sc[...])
    for g in range(pages // group):
        vc = jnp.concatenate([page_rows(r) for r in v_refs[g * group:(g + 1) * group]], axis=0)
        stats = _flash_step(scores[g], vc, *stats)
    m_sc[...], l_sc[...], acc_sc[...] = stats

    @pl.when(c == pl.num_programs(1) - 1)
    def _():
        kn = kn_ref[...]
        sn = lax.dot_general(q2, kn, _NT, preferred_element_type=F32)
        tk = lax.broadcasted_iota(jnp.int32, sn.shape, 1)
        tq = lax.broadcasted_iota(jnp.int32, sn.shape, 0) % T
        sn = jnp.where(tk <= tq, sn + slope * tk.astype(F32), -jnp.inf)
        _, l_fin, acc_fin = _flash_step(sn, vn_ref[...], m_sc[...], l_sc[...], acc_sc[...])
        lam = _lambda(lamv_ref, lam_init)
        for h in range(heads):
            blk = acc_fin[h * 2 * T:(h + 1) * 2 * T, h * vd:(h + 1) * vd] / l_fin[h * 2 * T:(h + 1) * 2 * T, :]
            o = blk[:T, :] - lam * blk[T:, :]
            o = o * lax.rsqrt(jnp.mean(o * o, axis=1, keepdims=True) + LN_EPS) * g_ref[...] * (1.0 - lam_init)
            o_ref[:, h * vd:(h + 1) * vd] = o.astype(o_ref.dtype)


def _attn_decode(q, kb_new, vb_new, cache_k, cache_v, page_table, layer, slopes, lamv, norm_g,
                 batch, T, heads, vd, lam_init):
    depth, n_phys, page = cache_k.shape[:3]
    n_pages = page_table.shape[1]
    past = n_pages * page
    pages = math.gcd(PAGES_PER_STEP, n_pages)
    aw = heads * vd
    qd = vd // 2
    rows = heads * 2 * T
    assert 2 * T == SUBLANES
    q3 = q.reshape(batch, 1, T, aw)
    colmap = jnp.arange(aw, dtype=jnp.int32) // qd
    rowmap = jnp.arange(heads * 2, dtype=jnp.int32)
    q2 = jnp.where((colmap[None, :] == rowmap[:, None])[None, :, None, :], q3, jnp.zeros_like(q3))
    q2 = q2.reshape(batch, rows, aw)
    slope_rows = jnp.repeat(slopes, 2 * T).reshape(rows, 1)
    pad = lambda a: jnp.pad(a.reshape(batch, T, aw), ((0, 0), (0, SUBLANES - T), (0, 0)))
    ck = cache_k.reshape(depth * n_phys * page * heads, vd)
    cv = cache_v.reshape(depth * n_phys * page * heads, vd)
    page_spec = lambda i: pl.BlockSpec(
        (page * heads, vd), lambda b, c, pt: (layer * n_phys + pt[b * n_pages + c * pages + i], 0))
    const = lambda a: pl.BlockSpec(a.shape, lambda b, c, pt: (0,) * a.ndim)
    per_b = lambda r: pl.BlockSpec((None, r, aw), lambda b, c, pt: (b, 0, 0))
    grid_spec = pltpu.PrefetchScalarGridSpec(
        num_scalar_prefetch=1,
        grid=(batch, n_pages // pages),
        in_specs=[per_b(rows), per_b(SUBLANES), per_b(SUBLANES), const(slope_rows), const(lamv), const(norm_g)]
                 + [page_spec(i) for i in range(pages)] + [page_spec(i) for i in range(pages)],
        out_specs=per_b(T),
        scratch_shapes=[pltpu.VMEM((rows, 1), F32), pltpu.VMEM((rows, 1), F32), pltpu.VMEM((rows, aw), F32)],
    )
    return pl.pallas_call(
        functools.partial(_attn_decode_kernel, pages=pages, page=page, past=past, heads=heads, vd=vd, T=T,
                          lam_init=lam_init),
        grid_spec=grid_spec,
        out_shape=jax.ShapeDtypeStruct((batch, T, aw), F32),
        compiler_params=_params("parallel", "arbitrary"),
        name="attn_decode",
    )(page_table.reshape(-1), q2, pad(kb_new), pad(vb_new), slope_rows, lamv, norm_g,
      *([ck] * pages), *([cv] * pages))


def _layer_norm(x, g, b):
    mu = jnp.mean(x, axis=1, keepdims=True)
    xc = x - mu
    var = jnp.mean(xc * xc, axis=1, keepdims=True)
    return xc * lax.rsqrt(var + LN_EPS) * g + b


def _post_kernel(hm_ref, ha_ref, x_ref, wom_ref, woa_ref, l1g_ref, l1b_ref, wg_ref, wu_ref, wd_ref,
                 l2g_ref, l2b_ref, y_ref, *, alpha, ff_chunks):
    mix = (jnp.dot(hm_ref[...].astype(BF16), wom_ref[...], preferred_element_type=F32)
           + jnp.dot(ha_ref[...].astype(BF16), woa_ref[...], preferred_element_type=F32))
    x1 = _layer_norm(alpha * x_ref[...] + mix, l1g_ref[...], l1b_ref[...])
    x1b = x1.astype(BF16)
    fc = wg_ref.shape[1] // ff_chunks
    ffn = None
    for f in range(ff_chunks):
        gate = jnp.dot(x1b, wg_ref[:, f * fc:(f + 1) * fc], preferred_element_type=F32)
        up = jnp.dot(x1b, wu_ref[:, f * fc:(f + 1) * fc], preferred_element_type=F32)
        mid = (gate * _sigmoid(gate) * up).astype(BF16)
        part = jnp.dot(mid, wd_ref[f * fc:(f + 1) * fc, :], preferred_element_type=F32)
        ffn = part if ffn is None else ffn + part
    y_ref[...] = _layer_norm(alpha * x1 + ffn, l2g_ref[...], l2b_ref[...])


def _post(hm, ha, x2d, w_out_m, w_out_a, ln1_g, ln1_b, w_gate, w_up, w_down, ln2_g, ln2_b, alpha):
    rows, d = x2d.shape
    tm = min(ROW_TILE, rows)
    assert rows % tm == 0
    d_ff = w_gate.shape[1]
    ff_chunks = d_ff // MXU_COLS if d_ff % MXU_COLS == 0 else 1
    row = lambda a: pl.BlockSpec((tm, a.shape[1]), lambda i: (i, 0))
    const = lambda a: pl.BlockSpec(a.shape, lambda i: (0,) * a.ndim, pipeline_mode=pl.Buffered(1))
    return pl.pallas_call(
        functools.partial(_post_kernel, alpha=alpha, ff_chunks=ff_chunks),
        grid=(rows // tm,),
        in_specs=[row(hm), row(ha), row(x2d), const(w_out_m), const(w_out_a), const(ln1_g), const(ln1_b),
                  const(w_gate), const(w_up), const(w_down), const(ln2_g), const(ln2_b)],
        out_specs=pl.BlockSpec((tm, d), lambda i: (i, 0)),
        out_shape=jax.ShapeDtypeStruct((rows, d), F32),
        compiler_params=_params("parallel"),
        name="post_ffn",
    )(hm, ha, x2d, w_out_m, w_out_a, ln1_g, ln1_b, w_gate, w_up, w_down, ln2_g, ln2_b)


def kernel(x_prompt, x_sample, cache_k, cache_v, page_table, state_C, state_n, state_m, state_conv, w_in, b_ig, b_fg, conv_w, conv_b, m_norm_g, lam_q1, lam_k1, lam_q2, lam_k2, a_norm_g, w_out, ln1_g, ln1_b, w_gate, w_up, w_down, ln2_g, ln2_b):
    B, S, D = x_prompt.shape
    DB, T, _ = x_sample.shape
    depth = w_in.shape[0]
    heads = b_ig.shape[1]
    mw = m_norm_g.shape[1]
    hd = mw // heads
    vd = a_norm_g.shape[1]
    aw = w_out.shape[1] - mw
    a_heads = aw // vd
    alpha = (2.0 * depth) ** 0.25
    slopes = jnp.exp2(-8.0 * jnp.arange(1, a_heads + 1, dtype=F32) / a_heads)
    q_scale = (vd // 2) ** -0.5

    hp = x_prompt.reshape(B * S, D)
    hs = x_sample.reshape(DB * T, D)
    outs = [[] for _ in range(12)]
    for l in range(depth):
        lam_init = 0.8 - 0.6 * math.exp(-0.3 * l)
        w = w_in[l]
        g_lo = 4 * mw
        g_hi = g_lo + 2 * heads
        w_main = jnp.concatenate([w[:, :g_lo], w[:, g_hi:]], axis=1).astype(BF16)
        w_gates = jnp.pad(w[:, g_lo:g_hi], ((0, 0), (0, LANES - 2 * heads))).astype(BF16)
        gates_b = jnp.concatenate([b_ig[l], b_fg[l]]).astype(F32)
        gb_row = jnp.pad(gates_b, (0, LANES - 2 * heads)).reshape(1, LANES)
        gb_col = jnp.pad(gates_b, (0, SUBLANES - 2 * heads)).reshape(SUBLANES, 1)
        cw = conv_w[l].astype(F32)
        cb = conv_b[l].astype(F32).reshape(1, -1)
        ng = m_norm_g[l].astype(F32).reshape(1, mw)
        ag = a_norm_g[l].astype(F32).reshape(1, vd)
        lamv = jnp.stack([lam_q1[l], lam_k1[l], lam_q2[l], lam_k2[l]]).astype(F32)
        wo = w_out[l].astype(BF16)
        post_w = (wo[:mw], wo[mw:], ln1_g[l].reshape(1, D), ln1_b[l].reshape(1, D),
                  w_gate[l].astype(BF16), w_up[l].astype(BF16), w_down[l].astype(BF16),
                  ln2_g[l].reshape(1, D), ln2_b[l].reshape(1, D))

        qk, vm, om, g, gt, qa, ka, va, kb, vb = _project(hp, w_main, w_gates, mw, aw, a_heads, q_scale, S)
        hm, Cp, n_p, mp, tail = _mlstm_prompt(qk, vm, om, g, gt, cw, cb, gb_row, gb_col, ng, B, S, heads, hd)
        ha = _attn_prompt(qa, kb, vb, slopes, lamv, ag, B, S, a_heads, vd, lam_init)
        hp = _post(hm.reshape(B * S, mw), ha.reshape(B * S, aw), hp, *post_w, alpha)
        hist = cw.shape[0] - 1
        outs[0].append(ka.reshape(B, S, a_heads, vd))
        outs[1].append(va.reshape(B, S, a_heads, vd))
        outs[4].append(Cp)
        outs[5].append(n_p)
        outs[6].append(mp[:, :, 0])
        outs[7].append(tail[:, SUBLANES - hist:, :])

        qk, vm, om, g, gt, qa, ka, va, kb, vb = _project(hs, w_main, w_gates, mw, aw, a_heads, q_scale, DB * T)
        hm, Cs, n_s, ms = _mlstm_sample(qk, vm, om, g, gt, state_conv, state_C, state_n, state_m, l,
                                        cw, cb, gb_row, gb_col, ng, DB, T, heads, hd)
        ha = _attn_decode(qa, kb, vb, cache_k, cache_v, page_table, l, slopes, lamv, ag,
                          DB, T, a_heads, vd, lam_init)
        hs = _post(hm.reshape(DB * T, mw), ha.reshape(DB * T, aw), hs, *post_w, alpha)
        outs[2].append(ka.reshape(DB, T, a_heads, vd))
        outs[3].append(va.reshape(DB, T, a_heads, vd))
        outs[8].append(Cs)
        outs[9].append(n_s)
        outs[10].append(ms[:, :, 0])
        outs[11].append(qk.reshape(DB, T, 2 * mw)[:, T - hist:, :])

    stack = lambda o: o[0][None] if depth == 1 else jnp.stack(o)
    return (hp.reshape(B, S, D), hs.reshape(DB, T, D), *[stack(o) for o in outs])
```
